```python
import math
import jax, jax.numpy as jnp
from jax import lax
import numpy as np

D_MODEL = 1024
BATCH = 4
SEQ = 8192
DEPTH = 2
DEC_BATCH = 32
DEC_SEQ = 1
PAST_LEN = 16384
PAGE_SIZE = 128

N_A = DEPTH // 2
N_B = DEPTH - N_A
N_DENSE = (DEPTH + 1) // 2
N_MOE = DEPTH // 2
POOL_WINDOWS = (2, 4, 8, 16)
N_POOL_GROUPS = 4
POOL_GROUP = D_MODEL // N_POOL_GROUPS
POOL_STATE = max(POOL_WINDOWS) - 1
N_HEADS = 16
HEAD_DIM = D_MODEL // N_HEADS
Q_BLOCK = 128
ATTN_SCALE = 1.0 / math.sqrt(HEAD_DIM)
D_FF = ((8 * D_MODEL // 3 + 127) // 128) * 128
N_EXPERTS = 8
TOP_K = 2
D_FF_EXPERT = 7 * D_MODEL // 2
PLE_DIM = 256
DN_ALPHA = (2.0 * DEPTH) ** 0.25
DN_BETA = (8.0 * DEPTH) ** -0.25
LN_EPS = 1e-5
FORGET_BIAS = 3.0

kernel_name = "yoco_pool_fox_moe_decoder_step"


def _layer_norm(x, g, b):
    xf = x.astype(jnp.float32)
    mu = jnp.mean(xf, axis=-1, keepdims=True)
    var = jnp.mean(jnp.square(xf - mu), axis=-1, keepdims=True)
    return ((xf - mu) * lax.rsqrt(var + LN_EPS) * g + b).astype(x.dtype)


def _pool_mix(x, prefix, pos0, w_groups, scale):
    b, t, _ = x.shape
    ext = jnp.concatenate([prefix, x], axis=1)
    cs = jnp.cumsum(ext.astype(jnp.float32), axis=1)
    cs0 = jnp.concatenate([jnp.zeros((b, 1, D_MODEL), jnp.float32), cs], axis=1)
    pos = pos0 + jnp.arange(t, dtype=jnp.int32)
    base = POOL_STATE + 1
    means = []
    for g, w in enumerate(POOL_WINDOWS):
        lo, hi = g * POOL_GROUP, (g + 1) * POOL_GROUP
        s = cs0[:, base:base + t, lo:hi] - cs0[:, base - w:base - w + t, lo:hi]
        cnt = jnp.minimum(pos + 1, w).astype(jnp.float32)
        means.append(s / cnt[None, :, None])
    mean = jnp.stack(means, axis=2).astype(x.dtype)
    diff = mean - x.reshape(b, t, N_POOL_GROUPS, POOL_GROUP)
    y = jnp.einsum('btgc,gcd->btgd', diff, w_groups).reshape(b, t, D_MODEL)
    return y * scale, ext[:, -POOL_STATE:]


def _shared_kv(x, w_k, w_v, w_f, b_f):
    b, t, _ = x.shape
    k = (x @ w_k).reshape(b, t, N_HEADS, HEAD_DIM)
    v = (x @ w_v).reshape(b, t, N_HEADS, HEAD_DIM)
    logf = jax.nn.log_sigmoid((x @ w_f + b_f).astype(jnp.float32))
    return k, v, logf


def _fox_prompt(q, k, v, logf):
    b, s, h, d = q.shape
    ct = jnp.cumsum(logf, axis=1).transpose(0, 2, 1)
    kidx = jnp.arange(s)

    def block(i):
        start = i * Q_BLOCK
        q_blk = lax.dynamic_slice_in_dim(q, start, Q_BLOCK, axis=1)
        c_q = lax.dynamic_slice_in_dim(ct, start, Q_BLOCK, axis=2)
        qidx = start + jnp.arange(Q_BLOCK)
        logits = (jnp.einsum('bqhd,bkhd->bhqk', q_blk, k).astype(jnp.float32) * ATTN_SCALE
                  + (c_q[..., :, None] - ct[..., None, :]))
        logits = jnp.where(kidx[None, :] <= qidx[:, None], logits, -jnp.inf)
        p = jax.nn.softmax(logits, axis=-1)
        return jnp.einsum('bhqk,bkhd->bqhd', p.astype(v.dtype), v)

    out = lax.map(block, jnp.arange(s // Q_BLOCK))
    return out.transpose(1, 0, 2, 3, 4).reshape(b, s, h * d)


def _fox_sample(q, k_new, v_new, lf_new, cache_k, cache_v, cache_logf, page_table):
    b, tn, h, d = q.shape
    past = page_table.shape[1] * PAGE_SIZE
    kp = cache_k[page_table].reshape(b, past, h, d)
    vp = cache_v[page_table].reshape(b, past, h, d)
    lfp = cache_logf[page_table].reshape(b, past, h).astype(jnp.float32)
    ct = jnp.cumsum(jnp.concatenate([lfp, lf_new], axis=1), axis=1).transpose(0, 2, 1)
    c_q = ct[:, :, past:]
    s_past = (jnp.einsum('bqhd,bkhd->bhqk', q, kp).astype(jnp.float32) * ATTN_SCALE
              + c_q[..., :, None] - ct[:, :, None, :past])
    s_new = (jnp.einsum('bqhd,bkhd->bhqk', q, k_new).astype(jnp.float32) * ATTN_SCALE
             + c_q[..., :, None] - c_q[..., None, :])
    causal = jnp.tril(jnp.ones((tn, tn), dtype=bool))
    s_new = jnp.where(causal, s_new, -jnp.inf)
    p = jax.nn.softmax(jnp.concatenate([s_past, s_new], axis=-1), axis=-1)
    out = (jnp.einsum('bhqk,bkhd->bqhd', p[..., :past].astype(vp.dtype), vp)
           + jnp.einsum('bhqk,bkhd->bqhd', p[..., past:].astype(v_new.dtype), v_new))
    return out.reshape(b, tn, h * d)


def _swiglu(x, w1, w3, w2):
    return (jax.nn.silu(x @ w1) * (x @ w3)) @ w2


def _moe_swiglu(x, router_w, w1, w3, w2):
    shp = x.shape
    xt = x.reshape(-1, D_MODEL)
    logits = (xt @ router_w).astype(jnp.float32)
    top_v, top_i = lax.top_k(logits, TOP_K)
    gates = jax.nn.softmax(top_v, axis=-1)
    combine = jnp.sum(jax.nn.one_hot(top_i, N_EXPERTS, dtype=jnp.float32) * gates[..., None], axis=1)
    out = jnp.zeros_like(xt)
    for e in range(N_EXPERTS):
        out = out + combine[:, e:e + 1].astype(xt.dtype) * _swiglu(xt, w1[e], w3[e], w2[e])
    return out.reshape(shp)


def setup_inputs(seed: int = 0) -> dict:
    key = jax.random.key(seed)
    ks = jax.random.split(key, 32)
    f32 = jnp.float32
    n_pages = PAST_LEN // PAGE_SIZE
    n_used = DEC_BATCH * n_pages
    n_pool = n_used + max(1, n_used // 4)

    def nrm(k, shape, scale):
        return jax.random.normal(k, shape, f32) * scale

    hd = N_HEADS * HEAD_DIM
    page_table = jax.random.permutation(ks[0], n_pool)[:n_used].reshape(DEC_BATCH, n_pages).astype(jnp.int32)
    return {
        "x_prompt": nrm(ks[1], (BATCH, SEQ, D_MODEL), 1.0),
        "x_sample": nrm(ks[2], (DEC_BATCH, DEC_SEQ, D_MODEL), 1.0),
        "state_pool": nrm(ks[3], (N_A, DEC_BATCH, POOL_STATE, D_MODEL), 1.0),
        "cache_k": nrm(ks[4], (n_pool, PAGE_SIZE, N_HEADS, HEAD_DIM), 1.0),
        "cache_v": nrm(ks[5], (n_pool, PAGE_SIZE, N_HEADS, HEAD_DIM), DN_BETA),
        "cache_logf": jax.nn.log_sigmoid(FORGET_BIAS + nrm(ks[6], (n_pool, PAGE_SIZE, N_HEADS), 0.5)),
        "page_table": page_table,
        "p_prompt": nrm(ks[7], (DEPTH, BATCH, SEQ, PLE_DIM), 1.0),
        "p_sample": nrm(ks[8], (DEPTH, DEC_BATCH, DEC_SEQ, PLE_DIM), 1.0),
        "ln_g": 1.0 + nrm(ks[9], (DEPTH, 2, D_MODEL), 0.05),
        "ln_b": nrm(ks[10], (DEPTH, 2, D_MODEL), 0.02),
        "pool_w": nrm(ks[11], (N_A, N_POOL_GROUPS, POOL_GROUP, POOL_GROUP), POOL_GROUP ** -0.5 * DN_BETA),
        "pool_scale": 1.0 + nrm(ks[12], (N_A, D_MODEL), 0.1),
        "w_q": nrm(ks[13], (N_B, D_MODEL, hd), D_MODEL ** -0.5),
        "w_o": nrm(ks[14], (N_B, hd, D_MODEL), hd ** -0.5 * DN_BETA),
        "w_k": nrm(ks[15], (D_MODEL, hd), D_MODEL ** -0.5),
        "w_v": nrm(ks[16], (D_MODEL, hd), D_MODEL ** -0.5 * DN_BETA),
        "w_f": nrm(ks[17], (D_MODEL, N_HEADS), 0.5 * D_MODEL ** -0.5),
        "b_f": FORGET_BIAS + nrm(ks[18], (N_HEADS,), 0.5),
        "ffn_w1": nrm(ks[19], (N_DENSE, D_MODEL, D_FF), D_MODEL ** -0.5),
        "ffn_w3": nrm(ks[20], (N_DENSE, D_MODEL, D_FF), D_MODEL ** -0.5),
        "ffn_w2": nrm(ks[21], (N_DENSE, D_FF, D_MODEL), D_FF ** -0.5 * DN_BETA),
        "router_w": nrm(ks[22], (N_MOE, D_MODEL, N_EXPERTS), D_MODEL ** -0.5),
        "moe_w1": nrm(ks[23], (N_MOE, N_EXPERTS, D_MODEL, D_FF_EXPERT), D_MODEL ** -0.5),
        "moe_w3": nrm(ks[24], (N_MOE, N_EXPERTS, D_MODEL, D_FF_EXPERT), D_MODEL ** -0.5),
        "moe_w2": nrm(ks[25], (N_MOE, N_EXPERTS, D_FF_EXPERT, D_MODEL), D_FF_EXPERT ** -0.5 * DN_BETA),
        "ple_proj": nrm(ks[26], (DEPTH, PLE_DIM, D_MODEL), PLE_DIM ** -0.5 * DN_BETA),
        "ple_gate": nrm(ks[27], (DEPTH, D_MODEL, D_MODEL), D_MODEL ** -0.5),
    }


def reference(x_prompt, x_sample, state_pool, cache_k, cache_v, cache_logf, page_table,
              p_prompt, p_sample, ln_g, ln_b, pool_w, pool_scale, w_q, w_o, w_k, w_v, w_f, b_f,
              ffn_w1, ffn_w3, ffn_w2, router_w, moe_w1, moe_w3, moe_w2, ple_proj, ple_gate):
    xp, xs = x_prompt, x_sample
    bp, sp = xp.shape[0], xp.shape[1]
    bs, ts = xs.shape[0], xs.shape[1]
    pool_p, pool_s = [], []
    k_p = v_p = lf_p = k_s = v_s = lf_s = None
    for layer in range(DEPTH):
        if layer < N_A:
            a = layer
            yp, st_p = _pool_mix(xp, jnp.zeros((bp, POOL_STATE, D_MODEL), xp.dtype), 0, pool_w[a], pool_scale[a])
            ys, st_s = _pool_mix(xs, state_pool[a], PAST_LEN, pool_w[a], pool_scale[a])
            pool_p.append(st_p)
            pool_s.append(st_s)
        else:
            bl = layer - N_A
            if layer == N_A:
                k_p, v_p, lf_p = _shared_kv(xp, w_k, w_v, w_f, b_f)
                k_s, v_s, lf_s = _shared_kv(xs, w_k, w_v, w_f, b_f)
            q_p = (xp @ w_q[bl]).reshape(bp, sp, N_HEADS, HEAD_DIM)
            q_s = (xs @ w_q[bl]).reshape(bs, ts, N_HEADS, HEAD_DIM)
            yp = _fox_prompt(q_p, k_p, v_p, lf_p) @ w_o[bl]
            ys = _fox_sample(q_s, k_s, v_s, lf_s, cache_k, cache_v, cache_logf, page_table) @ w_o[bl]
        xp = _layer_norm(DN_ALPHA * xp + yp, ln_g[layer, 0], ln_b[layer, 0])
        xs = _layer_norm(DN_ALPHA * xs + ys, ln_g[layer, 0], ln_b[layer, 0])
        j = layer // 2
        if layer % 2 == 0:
            fp = _swiglu(xp, ffn_w1[j], ffn_w3[j], ffn_w2[j])
            fs = _swiglu(xs, ffn_w1[j], ffn_w3[j], ffn_w2[j])
        else:
            fp = _moe_swiglu(xp, router_w[j], moe_w1[j], moe_w3[j], moe_w2[j])
            fs = _moe_swiglu(xs, router_w[j], moe_w1[j], moe_w3[j], moe_w2[j])
        rp = DN_ALPHA * xp + fp
        rs = DN_ALPHA * xs + fs
        rp = rp + jax.nn.sigmoid(rp @ ple_gate[layer]) * (p_prompt[layer] @ ple_proj[layer])
        rs = rs + jax.nn.sigmoid(rs @ ple_gate[layer]) * (p_sample[layer] @ ple_proj[layer])
        xp = _layer_norm(rp, ln_g[layer, 1], ln_b[layer, 1])
        xs = _layer_norm(rs, ln_g[layer, 1], ln_b[layer, 1])
    return (xp, xs, jnp.stack(pool_p), jnp.stack(pool_s), k_p, v_p, lf_p, k_s, v_s, lf_s)
```

```python
import functools
import math

import jax
import jax.numpy as jnp
from jax import lax
from jax.experimental import pallas as pl
from jax.experimental.pallas import tpu as pltpu

F32 = jnp.float32
BF16 = jnp.bfloat16

N_HEADS = 16
HEAD_DIM = 64
HEADS_PER_LANE_BLOCK = 2
LANES = 128
POOL_WINDOWS = (2, 4, 8, 16)
POOL_STATE = max(POOL_WINDOWS) - 1
POOL_HALO = 16
DEPTH = 2
DN_ALPHA = (2.0 * DEPTH) ** 0.25
LN_EPS = 1e-5
ATTN_SCALE = 1.0 / math.sqrt(HEAD_DIM)
TOP_K = 2
N_BIAS_PIECES = 3
NEG_INF = float("-inf")
VMEM_LIMIT = 56 * 1024 * 1024


def _cparams(*sem):
    return pltpu.CompilerParams(dimension_semantics=sem, vmem_limit_bytes=VMEM_LIMIT)


def _resident(shape):
    nd = len(shape)
    return pl.BlockSpec(shape, lambda *_: (0,) * nd, pipeline_mode=pl.Buffered(1))


def _dot(a, b):
    return jnp.dot(a, b, preferred_element_type=F32)


def _dot_f32(a, b):
    return jnp.dot(a, b, preferred_element_type=F32, precision=lax.Precision.HIGHEST)


def _layer_norm(x, g, b):
    mu = jnp.mean(x, axis=-1, keepdims=True)
    xc = x - mu
    var = jnp.mean(xc * xc, axis=-1, keepdims=True)
    return xc * lax.rsqrt(var + LN_EPS) * g + b


def _sigmoid(x):
    return 1.0 / (1.0 + jnp.exp(-x))


def _log_sigmoid(z):
    return jnp.minimum(z, 0.0) - jnp.log1p(jnp.exp(-jnp.abs(z)))


def _pick_chunk(n, cap):
    best = LANES
    for c in range(LANES, min(n, cap) + 1, LANES):
        if n % c == 0:
            best = c
    return best


def _pick_tile(n, cap):
    if n <= cap:
        return n
    for t in range(cap, 7, -8):
        if n % t == 0:
            return t
    raise ValueError(f"no tile for {n}")


def _pool_groups(ext_ref, x, pos, w_ref, row0):
    t = x.shape[0]
    group = x.shape[1] // len(POOL_WINDOWS)
    ys = []
    for g, w in enumerate(POOL_WINDOWS):
        lo, hi = g * group, (g + 1) * group
        s = ext_ref[row0:row0 + t, lo:hi]
        for j in range(1, w):
            s = s + ext_ref[row0 - j:row0 - j + t, lo:hi]
        cnt = jnp.minimum(pos + 1, w).astype(F32)
        diff = s / cnt - x[:, lo:hi]
        ys.append(_dot(diff.astype(BF16), w_ref[g]))
    return jnp.concatenate(ys, axis=1)


def _pool_ln_prompt_kernel(x_ref, prev_ref, w_ref, sc_ref, g_ref, b_ref, o_ref, ext_ref):
    t_idx = pl.program_id(1)
    ts = x_ref.shape[1]
    x = x_ref[0]
    ext_ref[0:POOL_HALO, :] = jnp.where(t_idx > 0, prev_ref[0], 0.0)
    ext_ref[POOL_HALO:POOL_HALO + ts, :] = x
    pos = lax.broadcasted_iota(jnp.int32, (ts, 1), 0) + t_idx * ts
    y = _pool_groups(ext_ref, x, pos, w_ref, POOL_HALO) * sc_ref[...]
    o_ref[0] = _layer_norm(DN_ALPHA * x + y, g_ref[...], b_ref[...])


def _pool_ln_prompt(x, pool_w, pool_scale, g, b):
    bsz, seq, d = x.shape
    ts = _pick_tile(seq, 512)
    halo_blocks = ts // POOL_HALO
    return pl.pallas_call(
        _pool_ln_prompt_kernel,
        out_shape=jax.ShapeDtypeStruct(x.shape, F32),
        grid=(bsz, seq // ts),
        in_specs=[
            pl.BlockSpec((1, ts, d), lambda bi, ti: (bi, ti, 0)),
            pl.BlockSpec((1, POOL_HALO, d), lambda bi, ti: (bi, jnp.maximum(ti * halo_blocks - 1, 0), 0)),
            _resident(pool_w.shape), _resident((1, d)), _resident((1, d)), _resident((1, d)),
        ],
        out_specs=pl.BlockSpec((1, ts, d), lambda bi, ti: (bi, ti, 0)),
        scratch_shapes=[pltpu.VMEM((POOL_HALO + ts, d), F32)],
        compiler_params=_cparams("arbitrary", "arbitrary"),
        name="pool_ln_prompt",
    )(x, x, pool_w, pool_scale.reshape(1, d), g.reshape(1, d), b.reshape(1, d))


def _pool_ln_sample_kernel(x_ref, st_ref, w_ref, sc_ref, g_ref, b_ref, o_ref, *, pos0):
    x = x_ref[...]
    group = x.shape[1] // len(POOL_WINDOWS)
    ys = []
    for g, w in enumerate(POOL_WINDOWS):
        lo, hi = g * group, (g + 1) * group
        s = x[:, lo:hi]
        for j in range(1, w):
            s = s + st_ref[POOL_STATE - j][:, lo:hi]
        cnt = float(min(pos0 + 1, w))
        diff = s / cnt - x[:, lo:hi]
        ys.append(_dot(diff.astype(BF16), w_ref[g]))
    y = jnp.concatenate(ys, axis=1) * sc_ref[...]
    o_ref[...] = _layer_norm(DN_ALPHA * x + y, g_ref[...], b_ref[...])


def _pool_ln_sample(xs, state_t, pool_w, pool_scale, g, b, pos0):
    n, d = xs.shape
    return pl.pallas_call(
        functools.partial(_pool_ln_sample_kernel, pos0=pos0),
        out_shape=jax.ShapeDtypeStruct((n, d), F32),
        compiler_params=pltpu.CompilerParams(vmem_limit_bytes=VMEM_LIMIT),
        name="pool_ln_sample",
    )(xs, state_t, pool_w, pool_scale.reshape(1, d), g.reshape(1, d), b.reshape(1, d))


def _swiglu(xb, w1_ref, w3_ref, w2_ref, ck, lead=()):
    dff = w1_ref.shape[-1]
    acc = None
    for c in range(dff // ck):
        sl = slice(c * ck, (c + 1) * ck)
        a = _dot(xb, w1_ref[lead + (slice(None), sl)])
        g = _dot(xb, w3_ref[lead + (slice(None), sl)])
        h = (a * _sigmoid(a) * g).astype(BF16)
        part = _dot(h, w2_ref[lead + (sl, slice(None))])
        acc = part if acc is None else acc + part
    return acc


def _ple_ln(r, p, gate_ref, proj_ref, g_ref, b_ref):
    gate = _sigmoid(_dot(r.astype(BF16), gate_ref[...]))
    r = r + gate * _dot(p.astype(BF16), proj_ref[...])
    return _layer_norm(r, g_ref[...], b_ref[...])


def _dense_ffn_kernel(x_ref, p_ref, w1_ref, w3_ref, w2_ref, gate_ref, proj_ref, g_ref, b_ref, o_ref, *, ck):
    x = x_ref[...]
    f = _swiglu(x.astype(BF16), w1_ref, w3_ref, w2_ref, ck)
    o_ref[...] = _ple_ln(DN_ALPHA * x + f, p_ref[...], gate_ref, proj_ref, g_ref, b_ref)


def _dense_ffn(x, p, w1, w3, w2, gate, proj, g, b):
    t, d = x.shape
    tm = _pick_tile(t, 512)
    ck = _pick_chunk(w1.shape[1], 1408)
    return pl.pallas_call(
        functools.partial(_dense_ffn_kernel, ck=ck),
        out_shape=jax.ShapeDtypeStruct((t, d), F32),
        grid=(t // tm,),
        in_specs=[
            pl.BlockSpec((tm, d), lambda i: (i, 0)),
            pl.BlockSpec((tm, p.shape[1]), lambda i: (i, 0)),
            _resident(w1.shape), _resident(w3.shape), _resident(w2.shape),
            _resident(gate.shape), _resident(proj.shape), _resident((1, d)), _resident((1, d)),
        ],
        out_specs=pl.BlockSpec((tm, d), lambda i: (i, 0)),
        compiler_params=_cparams("arbitrary"),
        name="dense_ffn",
    )(x, p, w1, w3, w2, gate, proj, g.reshape(1, d), b.reshape(1, d))


def _bias_pieces(c):
    pieces = []
    rem = c
    for _ in range(N_BIAS_PIECES - 1):
        hi = rem.astype(BF16).astype(F32)
        pieces.append(hi)
        rem = rem - hi
    pieces.append(rem)
    return pieces


def _kvq_prompt_kernel(x_ref, wk_ref, wv_ref, wq_ref, wf_ref, bf_ref,
                       k_ref, v_ref, lf_ref, qp_ref, kp_ref, vb_ref, carry_ref):
    t_idx = pl.program_id(1)
    tm = x_ref.shape[1]
    x = x_ref[0]
    xb = x.astype(BF16)
    k = _dot(xb, wk_ref[...])
    v = _dot(xb, wv_ref[...])
    q = _dot(xb, wq_ref[...]) * ATTN_SCALE
    k_ref[0] = k
    v_ref[0] = v
    vb_ref[0] = v.astype(BF16)
    logf = _log_sigmoid(_dot_f32(x, wf_ref[...]) + bf_ref[...])
    lf_ref[0] = logf

    @pl.when(t_idx == 0)
    def _():
        carry_ref[...] = jnp.zeros_like(carry_ref)

    row = lax.broadcasted_iota(jnp.int32, (tm, tm), 0)
    col = lax.broadcasted_iota(jnp.int32, (tm, tm), 1)
    tri = (col <= row).astype(F32)
    c = _dot_f32(tri, logf) + carry_ref[...]
    carry_ref[...] = c[tm - 1:tm, :]
    neg = _bias_pieces(-c)

    lane = lax.broadcasted_iota(jnp.int32, (tm, LANES), 1)
    for blk in range(N_HEADS // HEADS_PER_LANE_BLOCK):
        kb = k[:, blk * LANES:(blk + 1) * LANES]
        qb = q[:, blk * LANES:(blk + 1) * LANES]
        for par in range(HEADS_PER_LANE_BLOCK):
            h = blk * HEADS_PER_LANE_BLOCK + par
            base = HEAD_DIM if par == 0 else 0
            own = (lane < HEAD_DIM) if par == 0 else (lane >= HEAD_DIM)
            extra_k = jnp.zeros((tm, LANES), F32)
            extra_q = jnp.zeros((tm, LANES), F32)
            for i, piece in enumerate(neg):
                extra_k = jnp.where(lane == base + i, piece[:, h:h + 1], extra_k)
                extra_q = jnp.where(lane == base + i, 1.0, extra_q)
            kp_ref[0, h] = jnp.where(own, kb, extra_k).astype(BF16)
            qp_ref[0, h] = jnp.where(own, qb, extra_q).astype(BF16)


def _kvq_prompt(x, wk, wv, wq, wf, bf):
    bsz, seq, d = x.shape
    tm = _pick_tile(seq, 512)
    tok = lambda bi, ti: (bi, ti, 0)
    heads = lambda bi, ti: (bi, 0, ti, 0)
    return pl.pallas_call(
        _kvq_prompt_kernel,
        out_shape=[
            jax.ShapeDtypeStruct((bsz, seq, d), F32),
            jax.ShapeDtypeStruct((bsz, seq, d), F32),
            jax.ShapeDtypeStruct((bsz, seq, N_HEADS), F32),
            jax.ShapeDtypeStruct((bsz, N_HEADS, seq, LANES), BF16),
            jax.ShapeDtypeStruct((bsz, N_HEADS, seq, LANES), BF16),
            jax.ShapeDtypeStruct((bsz, seq, d), BF16),
        ],
        grid=(bsz, seq // tm),
        in_specs=[
            pl.BlockSpec((1, tm, d), tok),
            _resident(wk.shape), _resident(wv.shape), _resident(wq.shape),
            _resident(wf.shape), _resident((1, N_HEADS)),
        ],
        out_specs=[
            pl.BlockSpec((1, tm, d), tok), pl.BlockSpec((1, tm, d), tok),
            pl.BlockSpec((1, tm, N_HEADS), tok),
            pl.BlockSpec((1, N_HEADS, tm, LANES), heads), pl.BlockSpec((1, N_HEADS, tm, LANES), heads),
            pl.BlockSpec((1, tm, d), tok),
        ],
        scratch_shapes=[pltpu.VMEM((1, N_HEADS), F32)],
        compiler_params=_cparams("arbitrary", "arbitrary"),
        name="kvq_prompt",
    )(x, wk, wv, wq, wf, bf.reshape(1, N_HEADS))


def _kvq_sample_kernel(x_ref, wk_ref, wv_ref, wq_ref, wf_ref, bf_ref, k_ref, v_ref, lf_ref, q_ref):
    x = x_ref[...]
    xb = x.astype(BF16)
    k_ref[...] = _dot(xb, wk_ref[...])
    v_ref[...] = _dot(xb, wv_ref[...])
    q_ref[...] = _dot(xb, wq_ref[...]) * ATTN_SCALE
    lf_ref[...] = _log_sigmoid(_dot_f32(x, wf_ref[...]) + bf_ref[...])


def _kvq_sample(x, wk, wv, wq, wf, bf):
    n, d = x.shape
    return pl.pallas_call(
        _kvq_sample_kernel,
        out_shape=[jax.ShapeDtypeStruct((n, d), F32), jax.ShapeDtypeStruct((n, d), F32),
                   jax.ShapeDtypeStruct((n, N_HEADS), F32), jax.ShapeDtypeStruct((n, d), F32)],
        compiler_params=pltpu.CompilerParams(vmem_limit_bytes=VMEM_LIMIT),
        name="kvq_sample",
    )(x, wk, wv, wq, wf, bf.reshape(1, N_HEADS))


def _flash_kernel(q_ref, k_ref, v_ref, o_ref, *, tq):
    qi = pl.program_id(2)
    rowi = lax.broadcasted_iota(jnp.int32, (tq, tq), 0)
    coli = lax.broadcasted_iota(jnp.int32, (tq, tq), 1)
    lane = lax.broadcasted_iota(jnp.int32, (tq, LANES), 1)
    outs = []
    for par in range(HEADS_PER_LANE_BLOCK):
        q = q_ref[0, par]

        def step(kv, carry, masked, q=q, par=par):
            m, l, acc = carry
            off = pl.multiple_of(kv * tq, tq)
            k = k_ref[0, par, pl.ds(off, tq), :]
            v = v_ref[0, pl.ds(off, tq), :]
            s = lax.dot_general(q, k, (((1,), (1,)), ((), ())), preferred_element_type=F32)
            if masked:
                s = jnp.where(coli <= rowi, s, NEG_INF)
            m_new = jnp.maximum(m, jnp.max(s, axis=1, keepdims=True))
            alpha = jnp.exp(m - m_new)
            p = jnp.exp(s - m_new)
            l = alpha * l + jnp.sum(p, axis=1, keepdims=True)
            acc = alpha * acc + _dot(p.astype(BF16), v)
            return m_new, l, acc

        init = (jnp.full((tq, 1), NEG_INF, F32), jnp.zeros((tq, 1), F32), jnp.zeros((tq, LANES), F32))
        carry = lax.fori_loop(0, qi, functools.partial(step, masked=False), init)
        _, l, acc = step(qi, carry, True)
        outs.append(acc / l)
    o_ref[0] = jnp.where(lane < HEAD_DIM, outs[0], outs[1]).astype(o_ref.dtype)


def _flash_prompt(qp, kp, vb):
    bsz, _, seq, _ = qp.shape
    d = vb.shape[2]
    tq = _pick_tile(seq, 512)
    nblk = N_HEADS // HEADS_PER_LANE_BLOCK
    return pl.pallas_call(
        functools.partial(_flash_kernel, tq=tq),
        out_shape=jax.ShapeDtypeStruct((bsz, seq, d), BF16),
        grid=(bsz, nblk, seq // tq),
        in_specs=[
            pl.BlockSpec((1, HEADS_PER_LANE_BLOCK, tq, LANES), lambda bi, hb, qi: (bi, hb, qi, 0)),
            pl.BlockSpec((1, HEADS_PER_LANE_BLOCK, seq, LANES), lambda bi, hb, qi: (bi, hb, 0, 0)),
            pl.BlockSpec((1, seq, LANES), lambda bi, hb, qi: (bi, 0, hb)),
        ],
        out_specs=pl.BlockSpec((1, tq, LANES), lambda bi, hb, qi: (bi, qi, hb)),
        compiler_params=_cparams("arbitrary", "arbitrary", "arbitrary"),
        name="flash_prompt",
    )(qp, kp, vb)


def _decode_kernel(pt_ref, q_ref, kn_ref, vn_ref, lfn_ref, *refs, n_group):
    k_refs = refs[:n_group]
    v_refs = refs[n_group:2 * n_group]
    lf_refs = refs[2 * n_group:3 * n_group]
    o_ref = refs[3 * n_group]
    m_ref, l_ref, acc_ref, carry_ref = refs[3 * n_group + 1:]
    step = pl.program_id(1)
    page = k_refs[0].shape[1]
    d = q_ref.shape[2]

    head_of_lane = lax.broadcasted_iota(jnp.int32, (N_HEADS, d), 1) // HEAD_DIM
    head_row = lax.broadcasted_iota(jnp.int32, (N_HEADS, d), 0)
    own = head_of_lane == head_row
    qh = jnp.where(own, q_ref[0], 0.0)

    @pl.when(step == 0)
    def _():
        m_ref[...] = jnp.full_like(m_ref, NEG_INF)
        l_ref[...] = jnp.zeros_like(l_ref)
        acc_ref[...] = jnp.zeros_like(acc_ref)
        carry_ref[...] = lfn_ref[0]

    qb = qh.astype(BF16)
    jr = lax.broadcasted_iota(jnp.int32, (page, page), 0)
    kc = lax.broadcasted_iota(jnp.int32, (page, page), 1)
    later = (jr > kc).astype(F32)
    for g in range(n_group):
        kb = k_refs[g][0].astype(BF16)
        s = lax.dot_general(qb, kb, (((1,), (1,)), ((), ())), preferred_element_type=F32)
        lf = lf_refs[g][0]
        bias = _dot_f32(lf, later) + carry_ref[...]
        carry_ref[...] = carry_ref[...] + jnp.sum(lf, axis=1, keepdims=True)
        s = s + bias
        m = m_ref[...]
        m_new = jnp.maximum(m, jnp.max(s, axis=1, keepdims=True))
        alpha = jnp.exp(m - m_new)
        p = jnp.exp(s - m_new)
        l_ref[...] = alpha * l_ref[...] + jnp.sum(p, axis=1, keepdims=True)
        acc_ref[...] = alpha * acc_ref[...] + _dot(p.astype(BF16), v_refs[g][0].astype(BF16))
        m_ref[...] = m_new

    @pl.when(step == pl.num_programs(1) - 1)
    def _():
        kn = kn_ref[0].astype(BF16).astype(F32)
        s_new = jnp.sum(qb.astype(F32) * kn, axis=1, keepdims=True)
        m = m_ref[...]
        m_new = jnp.maximum(m, s_new)
        alpha = jnp.exp(m - m_new)
        p_new = jnp.exp(s_new - m_new).astype(BF16).astype(F32)
        l = alpha * l_ref[...] + p_new
        acc = alpha * acc_ref[...] + p_new * vn_ref[0].astype(BF16).astype(F32)
        out = jnp.where(own, acc / l, 0.0)
        o_ref[0] = jnp.sum(out, axis=0, keepdims=True)


def _decode_attention(q, k_new, v_new, lf_new, cache_k, cache_v, cache_lf_t, page_table):
    bsz, d = q.shape
    n_pages = page_table.shape[1]
    page = cache_k.shape[1]
    n_group = 8
    while n_pages % n_group:
        n_group //= 2
    steps = n_pages // n_group

    def page_map(g):
        return lambda bi, si, pt: (pt[bi, n_pages - 1 - (si * n_group + g)], 0, 0)

    row = lambda bi, si, pt: (bi, 0, 0)
    in_specs = [pl.BlockSpec((1, 1, d), row), pl.BlockSpec((1, 1, d), row), pl.BlockSpec((1, 1, d), row),
                pl.BlockSpec((1, N_HEADS, 1), row)]
    in_specs += [pl.BlockSpec((1, page, d), page_map(g)) for g in range(n_group)]
    in_specs += [pl.BlockSpec((1, page, d), page_map(g)) for g in range(n_group)]
    in_specs += [pl.BlockSpec((1, N_HEADS, page), page_map(g)) for g in range(n_group)]
    out = pl.pallas_call(
        functools.partial(_decode_kernel, n_group=n_group),
        out_shape=jax.ShapeDtypeStruct((bsz, 1, d), F32),
        grid_spec=pltpu.PrefetchScalarGridSpec(
            num_scalar_prefetch=1,
            grid=(bsz, steps),
            in_specs=in_specs,
            out_specs=pl.BlockSpec((1, 1, d), row),
            scratch_shapes=[pltpu.VMEM((N_HEADS, 1), F32), pltpu.VMEM((N_HEADS, 1), F32),
                            pltpu.VMEM((N_HEADS, d), F32), pltpu.VMEM((N_HEADS, 1), F32)],
        ),
        compiler_params=_cparams("arbitrary", "arbitrary"),
        name="decode_attention",
    )(page_table, q.reshape(bsz, 1, d), k_new.reshape(bsz, 1, d), v_new.reshape(bsz, 1, d),
      lf_new.reshape(bsz, N_HEADS, 1), *([cache_k] * n_group), *([cache_v] * n_group),
      *([cache_lf_t] * n_group))
    return out.reshape(bsz, d)


def _attn_out_router_kernel(x_ref, a_ref, wo_ref, g_ref, b_ref, rw_ref,
                            x3_ref, idx_ref, gate_ref, rank_ref, cnt_ref, carry_ref):
    i = pl.program_id(0)
    tm = x_ref.shape[0]
    n_exp = rw_ref.shape[1]
    y = _dot(a_ref[...].astype(BF16), wo_ref[...])
    x3 = _layer_norm(DN_ALPHA * x_ref[...] + y, g_ref[...], b_ref[...])
    x3_ref[...] = x3

    logits = _dot_f32(x3, rw_ref[...])
    eid = lax.broadcasted_iota(jnp.int32, (tm, n_exp), 1).astype(F32)
    m1 = jnp.max(logits, axis=1, keepdims=True)
    i1 = jnp.min(jnp.where(logits == m1, eid, float(n_exp)), axis=1, keepdims=True)
    rest = jnp.where(eid == i1, NEG_INF, logits)
    m2 = jnp.max(rest, axis=1, keepdims=True)
    i2 = jnp.min(jnp.where(rest == m2, eid, float(n_exp)), axis=1, keepdims=True)
    e2 = jnp.exp(m2 - m1)
    g1 = 1.0 / (1.0 + e2)
    first = lax.broadcasted_iota(jnp.int32, (tm, TOP_K), 1) == 0
    idx_ref[...] = jnp.where(first, i1, i2).astype(jnp.int32)
    gate_ref[...] = jnp.where(first, g1, e2 * g1)

    @pl.when(i == 0)
    def _():
        carry_ref[...] = jnp.zeros_like(carry_ref)

    oh1 = (eid == i1).astype(F32)
    oh2 = (eid == i2).astype(F32)
    both = oh1 + oh2
    row = lax.broadcasted_iota(jnp.int32, (tm, tm), 0)
    col = lax.broadcasted_iota(jnp.int32, (tm, tm), 1)
    before = (col < row).astype(BF16)
    excl = _dot(before, both.astype(BF16)) + carry_ref[...]
    r1 = jnp.sum(excl * oh1, axis=1, keepdims=True)
    r2 = jnp.sum(excl * oh2, axis=1, keepdims=True)
    rank_ref[...] = jnp.where(first, r1, r2).astype(jnp.int32)
    total = carry_ref[...] + jnp.sum(both, axis=0, keepdims=True)
    carry_ref[...] = total
    cnt_ref[...] = total.astype(jnp.int32)


def _attn_out_router(x, attn, wo, g, b, router_w):
    t, d = x.shape
    n_exp = router_w.shape[1]
    tm = _pick_tile(t, 512)
    tok = lambda i: (i, 0)
    return pl.pallas_call(
        _attn_out_router_kernel,
        out_shape=[jax.ShapeDtypeStruct((t, d), F32), jax.ShapeDtypeStruct((t, TOP_K), jnp.int32),
                   jax.ShapeDtypeStruct((t, TOP_K), F32), jax.ShapeDtypeStruct((t, TOP_K), jnp.int32),
                   jax.ShapeDtypeStruct((1, n_exp), jnp.int32)],
        grid=(t // tm,),
        in_specs=[pl.BlockSpec((tm, d), tok), pl.BlockSpec((tm, d), tok),
                  _resident(wo.shape), _resident((1, d)), _resident((1, d)), _resident(router_w.shape)],
        out_specs=[pl.BlockSpec((tm, d), tok), pl.BlockSpec((tm, TOP_K), tok), pl.BlockSpec((tm, TOP_K), tok),
                   pl.BlockSpec((tm, TOP_K), tok), pl.BlockSpec((1, n_exp), lambda i: (0, 0))],
        scratch_shapes=[pltpu.VMEM((1, n_exp), F32)],
        compiler_params=_cparams("arbitrary"),
        name="attn_out_router",
    )(x, attn, wo, g.reshape(1, d), b.reshape(1, d), router_w)


def _dispatch_kernel(pos_ref, x_ref, o_hbm, sem):
    tr = x_ref.shape[0]

    def issue(r, carry):
        for c in range(TOP_K):
            dst = pos_ref[0, 0, r * TOP_K + c]
            pltpu.make_async_copy(x_ref.at[pl.ds(r, 1)], o_hbm.at[pl.ds(dst, 1)], sem).start()
        return carry

    lax.fori_loop(0, tr, issue, 0)
    for _ in range(TOP_K):
        pltpu.make_async_copy(x_ref, o_hbm.at[pl.ds(0, tr)], sem).wait()


def _dispatch(x, pos):
    t, d = x.shape
    tr = _pick_tile(t, 256)
    pos3 = pos.reshape(t // tr, 1, tr * TOP_K)
    return pl.pallas_call(
        _dispatch_kernel,
        out_shape=jax.ShapeDtypeStruct((t * TOP_K, d), x.dtype),
        grid=(t // tr,),
        in_specs=[pl.BlockSpec((1, 1, tr * TOP_K), lambda i: (i, 0, 0), memory_space=pltpu.SMEM),
                  pl.BlockSpec((tr, d), lambda i: (i, 0))],
        out_specs=pl.BlockSpec(memory_space=pl.ANY),
        scratch_shapes=[pltpu.SemaphoreType.DMA],
        compiler_params=_cparams("arbitrary"),
        name="moe_dispatch",
    )(pos3, x)


def _expert_ffn_kernel(tile_ref, exp_ref, lo_ref, hi_ref, x_ref, w1_ref, w3_ref, w2_ref, o_ref, *, ck):
    del tile_ref, exp_ref
    vi = pl.program_id(0)
    tm = x_ref.shape[0]
    lo = lo_ref[vi]
    hi = hi_ref[vi]

    @pl.when(hi > lo)
    def _():
        y = _swiglu(x_ref[...].astype(BF16), w1_ref, w3_ref, w2_ref, ck, lead=(0,))
        row = lax.broadcasted_iota(jnp.int32, (tm, 1), 0)
        mine = (row >= lo) & (row < hi)

        @pl.when(lo == 0)
        def _():
            o_ref[...] = jnp.where(mine, y, 0.0)

        @pl.when(lo > 0)
        def _():
            o_ref[...] = jnp.where(mine, y, o_ref[...])


def _expert_ffn(xs, w1, w3, w2, visits, tm):
    vt, ve, vlo, vhi = visits
    n, d = xs.shape
    ck = _pick_chunk(w1.shape[2], 512)
    wmap = lambda i, vt, ve, vlo, vhi: (ve[i], 0, 0)
    tmap = lambda i, vt, ve, vlo, vhi: (vt[i], 0)
    return pl.pallas_call(
        functools.partial(_expert_ffn_kernel, ck=ck),
        out_shape=jax.ShapeDtypeStruct((n, d), F32),
        grid_spec=pltpu.PrefetchScalarGridSpec(
            num_scalar_prefetch=4,
            grid=(vt.shape[0],),
            in_specs=[pl.BlockSpec((tm, d), tmap),
                      pl.BlockSpec((1,) + w1.shape[1:], wmap, pipeline_mode=pl.Buffered(1)),
                      pl.BlockSpec((1,) + w3.shape[1:], wmap, pipeline_mode=pl.Buffered(1)),
                      pl.BlockSpec((1,) + w2.shape[1:], wmap, pipeline_mode=pl.Buffered(1))],
            out_specs=pl.BlockSpec((tm, d), tmap),
        ),
        compiler_params=_cparams("arbitrary"),
        name="expert_ffn",
    )(vt, ve, vlo, vhi, xs, w1, w3, w2)


def _dense_experts_kernel(x_ref, c_ref, w1_ref, w3_ref, w2_ref, o_ref, *, ck):
    e = pl.program_id(0)
    y = c_ref[0] * _swiglu(x_ref[...].astype(BF16), w1_ref, w3_ref, w2_ref, ck, lead=(0,))

    @pl.when(e == 0)
    def _():
        o_ref[...] = y

    @pl.when(e > 0)
    def _():
        o_ref[...] = o_ref[...] + y


def _dense_experts(x, combine_t, w1, w3, w2):
    n, d = x.shape
    n_exp = w1.shape[0]
    ck = _pick_chunk(w1.shape[2], 512)
    wmap = lambda e: (e, 0, 0)
    return pl.pallas_call(
        functools.partial(_dense_experts_kernel, ck=ck),
        out_shape=jax.ShapeDtypeStruct((n, d), F32),
        grid=(n_exp,),
        in_specs=[pl.BlockSpec((n, d), lambda e: (0, 0)), pl.BlockSpec((1, n, 1), wmap),
                  pl.BlockSpec((1,) + w1.shape[1:], wmap), pl.BlockSpec((1,) + w3.shape[1:], wmap),
                  pl.BlockSpec((1,) + w2.shape[1:], wmap)],
        out_specs=pl.BlockSpec((n, d), lambda e: (0, 0)),
        compiler_params=_cparams("arbitrary"),
        name="dense_experts",
    )(x, combine_t.reshape(n_exp, n, 1), w1, w3, w2)


def _combine_ple_ln_kernel(pos_ref, x_ref, gt_ref, p_ref, gate_ref, proj_ref, g_ref, b_ref, y_hbm,
                           o_ref, buf_ref, sem):
    tm = x_ref.shape[0]

    def issue(r, carry):
        for c in range(TOP_K):
            src = pos_ref[0, 0, r * TOP_K + c]
            pltpu.make_async_copy(y_hbm.at[pl.ds(src, 1)], buf_ref.at[c, pl.ds(r, 1)], sem).start()
        return carry

    lax.fori_loop(0, tm, issue, 0)
    for c in range(TOP_K):
        pltpu.make_async_copy(y_hbm.at[pl.ds(0, tm)], buf_ref.at[c], sem).wait()
    gt = gt_ref[...]
    f = None
    for c in range(TOP_K):
        part = gt[:, c:c + 1] * buf_ref[c]
        f = part if f is None else f + part
    o_ref[...] = _ple_ln(DN_ALPHA * x_ref[...] + f, p_ref[...], gate_ref, proj_ref, g_ref, b_ref)


def _combine_ple_ln(x, ys, pos, gates, p, gate_w, proj, g, b):
    t, d = x.shape
    tm = _pick_tile(t, 256)
    pos3 = pos.reshape(t // tm, 1, tm * TOP_K)
    tok = lambda i: (i, 0)
    return pl.pallas_call(
        _combine_ple_ln_kernel,
        out_shape=jax.ShapeDtypeStruct((t, d), F32),
        grid=(t // tm,),
        in_specs=[pl.BlockSpec((1, 1, tm * TOP_K), lambda i: (i, 0, 0), memory_space=pltpu.SMEM),
                  pl.BlockSpec((tm, d), tok), pl.BlockSpec((tm, TOP_K), tok),
                  pl.BlockSpec((tm, p.shape[1]), tok),
                  _resident(gate_w.shape), _resident(proj.shape), _resident((1, d)), _resident((1, d)),
                  pl.BlockSpec(memory_space=pl.ANY)],
        out_specs=pl.BlockSpec((tm, d), tok),
        scratch_shapes=[pltpu.VMEM((TOP_K, tm, d), F32), pltpu.SemaphoreType.DMA],
        compiler_params=_cparams("arbitrary"),
        name="moe_combine_ple_ln",
    )(pos3, x, gates, p, gate_w, proj, g.reshape(1, d), b.reshape(1, d), ys)


def _res_ple_ln_kernel(x_ref, f_ref, p_ref, gate_ref, proj_ref, g_ref, b_ref, o_ref):
    o_ref[...] = _ple_ln(DN_ALPHA * x_ref[...] + f_ref[...], p_ref[...], gate_ref, proj_ref, g_ref, b_ref)


def _res_ple_ln(x, f, p, gate_w, proj, g, b):
    n, d = x.shape
    return pl.pallas_call(
        _res_ple_ln_kernel,
        out_shape=jax.ShapeDtypeStruct((n, d), F32),
        compiler_params=pltpu.CompilerParams(vmem_limit_bytes=VMEM_LIMIT),
        name="res_ple_ln",
    )(x, f, p, gate_w, proj, g.reshape(1, d), b.reshape(1, d))


def _visit_tables(counts, n_rows, tm):
    n_exp = counts.shape[0]
    n_tiles = n_rows // tm
    n_vis = n_tiles + n_exp - 1
    end = jnp.cumsum(counts)
    start = end - counts
    first_tile = start // tm
    last_tile = jnp.where(counts > 0, (end - 1) // tm, first_tile - 1)
    n_e = last_tile - first_tile + 1
    vis_end = jnp.cumsum(n_e)
    vis_start = vis_end - n_e
    v = jnp.arange(n_vis, dtype=jnp.int32)
    e = jnp.sum((v[:, None] >= vis_end[None, :]).astype(jnp.int32), axis=1)
    valid = e < n_exp
    last_e = jnp.max(jnp.where(counts > 0, jnp.arange(n_exp, dtype=jnp.int32), 0))
    e = jnp.where(valid, e, last_e).astype(jnp.int32)
    tile = jnp.where(valid, first_tile[e] + (v - vis_start[e]), n_tiles - 1).astype(jnp.int32)
    lo = jnp.maximum(start[e], tile * tm) - tile * tm
    hi = jnp.minimum(end[e], (tile + 1) * tm) - tile * tm
    lo = jnp.where(valid, lo, 0).astype(jnp.int32)
    hi = jnp.where(valid, hi, 0).astype(jnp.int32)
    return tile, e, lo, hi


def kernel(x_prompt, x_sample, state_pool, cache_k, cache_v, cache_logf, page_table, p_prompt, p_sample,
           ln_g, ln_b, pool_w, pool_scale, w_q, w_o, w_k, w_v, w_f, b_f, ffn_w1, ffn_w3, ffn_w2,
           router_w, moe_w1, moe_w3, moe_w2, ple_proj, ple_gate):
    bp, sp, d = x_prompt.shape
    bs, ts, _ = x_sample.shape
    assert ts == 1 and d == N_HEADS * HEAD_DIM
    tp = bp * sp
    page = cache_k.shape[1]
    past_len = page_table.shape[1] * page
    bf = lambda w: w.astype(BF16)

    pw = bf(pool_w[0])
    x1p = _pool_ln_prompt(x_prompt, pw, pool_scale[0], ln_g[0, 0], ln_b[0, 0])
    xs0 = x_sample.reshape(bs, d)
    x1s = _pool_ln_sample(xs0, jnp.swapaxes(state_pool[0], 0, 1), pw, pool_scale[0], ln_g[0, 0], ln_b[0, 0],
                          past_len)
    pool_p = x_prompt[:, sp - POOL_STATE:, :][None]
    pool_s = jnp.concatenate([state_pool[0][:, 1:], x_sample], axis=1)[None]

    w1, w3, w2 = bf(ffn_w1[0]), bf(ffn_w3[0]), bf(ffn_w2[0])
    gate0, proj0 = bf(ple_gate[0]), bf(ple_proj[0])
    x2p = _dense_ffn(x1p.reshape(tp, d), p_prompt[0].reshape(tp, -1), w1, w3, w2, gate0, proj0,
                     ln_g[0, 1], ln_b[0, 1])
    x2s = _dense_ffn(x1s, p_sample[0].reshape(bs, -1), w1, w3, w2, gate0, proj0, ln_g[0, 1], ln_b[0, 1])

    wk, wv, wq, wo = bf(w_k), bf(w_v), bf(w_q[0]), bf(w_o[0])
    k_p, v_p, lf_p, qp, kp, vb = _kvq_prompt(x2p.reshape(bp, sp, d), wk, wv, wq, w_f, b_f)
    attn_p = _flash_prompt(qp, kp, vb).reshape(tp, d)

    k_s, v_s, lf_s, q_s = _kvq_sample(x2s, wk, wv, wq, w_f, b_f)
    n_pool = cache_k.shape[0]
    attn_s = _decode_attention(q_s, k_s, v_s, lf_s, cache_k.reshape(n_pool, page, d),
                               cache_v.reshape(n_pool, page, d), jnp.swapaxes(cache_logf, 1, 2), page_table)

    gate1, proj1 = bf(ple_gate[1]), bf(ple_proj[1])
    m1, m3, m2 = bf(moe_w1[0]), bf(moe_w3[0]), bf(moe_w2[0])
    n_exp = m1.shape[0]

    x3p, idx_p, gates_p, rank_p, counts = _attn_out_router(x2p, attn_p, wo, ln_g[1, 0], ln_b[1, 0], router_w[0])
    counts = counts.reshape(n_exp)
    offsets = jnp.cumsum(counts) - counts
    pos = offsets[idx_p] + rank_p
    xs_sorted = _dispatch(x3p, pos)
    tm_moe = _pick_tile(tp * TOP_K, 512)
    ys_sorted = _expert_ffn(xs_sorted, m1, m3, m2, _visit_tables(counts, tp * TOP_K, tm_moe), tm_moe)
    y_p = _combine_ple_ln(x3p, ys_sorted, pos, gates_p, p_prompt[1].reshape(tp, -1), gate1, proj1,
                          ln_g[1, 1], ln_b[1, 1])

    x3s, idx_s, gates_s, _, _ = _attn_out_router(x2s, attn_s, wo, ln_g[1, 0], ln_b[1, 0], router_w[0])
    combine_t = jnp.sum(jnp.where(jnp.arange(n_exp)[:, None, None] == idx_s[None], gates_s[None], 0.0), axis=2)
    f_s = _dense_experts(x3s, combine_t, m1, m3, m2)
    y_s = _res_ple_ln(x3s, f_s, p_sample[1].reshape(bs, -1), gate1, proj1, ln_g[1, 1], ln_b[1, 1])

    hd = (N_HEADS, HEAD_DIM)
    return (y_p.reshape(bp, sp, d), y_s.reshape(bs, 1, d), pool_p, pool_s,
            k_p.reshape(bp, sp, *hd), v_p.reshape(bp, sp, *hd), lf_p,
            k_s.reshape(bs, 1, *hd), v_s.reshape(bs, 1, *hd), lf_s.reshape(bs, 1, N_HEADS))
```

```python
import functools
import math

import jax
import jax.numpy as jnp
from jax import lax
from jax.experimental import pallas as pl
from jax.experimental.pallas import tpu as pltpu

F32 = jnp.float32
BF16 = jnp.bfloat16

N_HEADS = 16
HEAD_DIM = 64
HEADS_PER_LANE_BLOCK = 2
LANES = 128
POOL_WINDOWS = (2, 4, 8, 16)
POOL_STATE = max(POOL_WINDOWS) - 1
POOL_HALO = 16
DEPTH = 2
DN_ALPHA = (2.0 * DEPTH) ** 0.25
LN_EPS = 1e-5
ATTN_SCALE = 1.0 / math.sqrt(HEAD_DIM)
TOP_K = 2
N_BIAS_PIECES = 3
NEG_INF = float("-inf")
VMEM_LIMIT = 56 * 1024 * 1024


def _cparams(*sem):
    return pltpu.CompilerParams(dimension_semantics=sem, vmem_limit_bytes=VMEM_LIMIT)


def _resident(shape):
    nd = len(shape)
    return pl.BlockSpec(shape, lambda *_: (0,) * nd, pipeline_mode=pl.Buffered(1))


def _dot(a, b):
    return jnp.dot(a, b, preferred_element_type=F32)


def _split_bf16(x, n):
    pieces = []
    rem = x
    for _ in range(n):
        hi = rem.astype(BF16)
        pieces.append(hi)
        rem = rem - hi.astype(F32)
    return pieces


def _dot_x3(x, w_hi, w_lo):
    x_hi, x_lo = _split_bf16(x, 2)
    return _dot(x_hi, w_hi) + (_dot(x_hi, w_lo) + _dot(x_lo, w_hi))


def _dot_mask(mask_bf16, x):
    parts = [_dot(mask_bf16, piece) for piece in _split_bf16(x, 3)]
    return (parts[0] + parts[1]) + parts[2]


def _split_weight(w):
    hi = w.astype(BF16)
    return hi, (w - hi.astype(F32)).astype(BF16)


def _layer_norm(x, g, b):
    mu = jnp.mean(x, axis=-1, keepdims=True)
    xc = x - mu
    var = jnp.mean(xc * xc, axis=-1, keepdims=True)
    return xc * lax.rsqrt(var + LN_EPS) * g + b


def _sigmoid(x):
    return 1.0 / (1.0 + jnp.exp(-x))


def _log_sigmoid(z):
    return jnp.minimum(z, 0.0) - jnp.log1p(jnp.exp(-jnp.abs(z)))


def _pick_chunk(n, cap):
    best = LANES
    for c in range(LANES, min(n, cap) + 1, LANES):
        if n % c == 0:
            best = c
    return best


def _pick_tile(n, cap):
    if n <= cap:
        return n
    for t in range(cap, 7, -8):
        if n % t == 0:
            return t
    raise ValueError(f"no tile for {n}")


def _pool_groups(ext_ref, x, pos, w_ref, row0):
    t = x.shape[0]
    group = x.shape[1] // len(POOL_WINDOWS)
    ys = []
    for g, w in enumerate(POOL_WINDOWS):
        lo, hi = g * group, (g + 1) * group
        s = ext_ref[row0:row0 + t, lo:hi]
        for j in range(1, w):
            s = s + ext_ref[row0 - j:row0 - j + t, lo:hi]
        cnt = jnp.minimum(pos + 1, w).astype(F32)
        diff = s / cnt - x[:, lo:hi]
        ys.append(_dot(diff.astype(BF16), w_ref[g]))
    return jnp.concatenate(ys, axis=1)


def _pool_ln_prompt_kernel(x_ref, prev_ref, w_ref, sc_ref, g_ref, b_ref, o_ref, ext_ref):
    t_idx = pl.program_id(1)
    ts = x_ref.shape[1]
    x = x_ref[0]
    ext_ref[0:POOL_HALO, :] = jnp.where(t_idx > 0, prev_ref[0], 0.0)
    ext_ref[POOL_HALO:POOL_HALO + ts, :] = x
    pos = lax.broadcasted_iota(jnp.int32, (ts, 1), 0) + t_idx * ts
    y = _pool_groups(ext_ref, x, pos, w_ref, POOL_HALO) * sc_ref[...]
    o_ref[0] = _layer_norm(DN_ALPHA * x + y, g_ref[...], b_ref[...])


def _pool_ln_prompt(x, pool_w, pool_scale, g, b):
    bsz, seq, d = x.shape
    ts = _pick_tile(seq, 512)
    halo_blocks = ts // POOL_HALO
    return pl.pallas_call(
        _pool_ln_prompt_kernel,
        out_shape=jax.ShapeDtypeStruct(x.shape, F32),
        grid=(bsz, seq // ts),
        in_specs=[
            pl.BlockSpec((1, ts, d), lambda bi, ti: (bi, ti, 0)),
            pl.BlockSpec((1, POOL_HALO, d), lambda bi, ti: (bi, jnp.maximum(ti * halo_blocks - 1, 0), 0)),
            _resident(pool_w.shape), _resident((1, d)), _resident((1, d)), _resident((1, d)),
        ],
        out_specs=pl.BlockSpec((1, ts, d), lambda bi, ti: (bi, ti, 0)),
        scratch_shapes=[pltpu.VMEM((POOL_HALO + ts, d), F32)],
        compiler_params=_cparams("arbitrary", "arbitrary"),
        name="pool_ln_prompt",
    )(x, x, pool_w, pool_scale.reshape(1, d), g.reshape(1, d), b.reshape(1, d))


def _pool_ln_sample_kernel(x_ref, st_ref, w_ref, sc_ref, g_ref, b_ref, o_ref, *, pos0):
    x = x_ref[...]
    group = x.shape[1] // len(POOL_WINDOWS)
    ys = []
    for g, w in enumerate(POOL_WINDOWS):
        lo, hi = g * group, (g + 1) * group
        s = x[:, lo:hi]
        for j in range(1, w):
            s = s + st_ref[POOL_STATE - j][:, lo:hi]
        cnt = float(min(pos0 + 1, w))
        diff = s / cnt - x[:, lo:hi]
        ys.append(_dot(diff.astype(BF16), w_ref[g]))
    y = jnp.concatenate(ys, axis=1) * sc_ref[...]
    o_ref[...] = _layer_norm(DN_ALPHA * x + y, g_ref[...], b_ref[...])


def _pool_ln_sample(xs, state_t, pool_w, pool_scale, g, b, pos0):
    n, d = xs.shape
    return pl.pallas_call(
        functools.partial(_pool_ln_sample_kernel, pos0=pos0),
        out_shape=jax.ShapeDtypeStruct((n, d), F32),
        compiler_params=pltpu.CompilerParams(vmem_limit_bytes=VMEM_LIMIT),
        name="pool_ln_sample",
    )(xs, state_t, pool_w, pool_scale.reshape(1, d), g.reshape(1, d), b.reshape(1, d))


def _swiglu(xb, w1_ref, w3_ref, w2_ref, ck, lead=()):
    dff = w1_ref.shape[-1]
    acc = None
    for c in range(dff // ck):
        sl = slice(c * ck, (c + 1) * ck)
        a = _dot(xb, w1_ref[lead + (slice(None), sl)])
        g = _dot(xb, w3_ref[lead + (slice(None), sl)])
        h = (a * _sigmoid(a) * g).astype(BF16)
        part = _dot(h, w2_ref[lead + (sl, slice(None))])
        acc = part if acc is None else acc + part
    return acc


def _ple_ln(r, p, gate_ref, proj_ref, g_ref, b_ref):
    gate = _sigmoid(_dot(r.astype(BF16), gate_ref[...]))
    r = r + gate * _dot(p.astype(BF16), proj_ref[...])
    return _layer_norm(r, g_ref[...], b_ref[...])


def _dense_ffn_kernel(x_ref, p_ref, w1_ref, w3_ref, w2_ref, gate_ref, proj_ref, g_ref, b_ref, o_ref, *, ck):
    x = x_ref[...]
    f = _swiglu(x.astype(BF16), w1_ref, w3_ref, w2_ref, ck)
    o_ref[...] = _ple_ln(DN_ALPHA * x + f, p_ref[...], gate_ref, proj_ref, g_ref, b_ref)


def _dense_ffn(x, p, w1, w3, w2, gate, proj, g, b):
    t, d = x.shape
    tm = _pick_tile(t, 512)
    ck = _pick_chunk(w1.shape[1], 1408)
    return pl.pallas_call(
        functools.partial(_dense_ffn_kernel, ck=ck),
        out_shape=jax.ShapeDtypeStruct((t, d), F32),
        grid=(t // tm,),
        in_specs=[
            pl.BlockSpec((tm, d), lambda i: (i, 0)),
            pl.BlockSpec((tm, p.shape[1]), lambda i: (i, 0)),
            _resident(w1.shape), _resident(w3.shape), _resident(w2.shape),
            _resident(gate.shape), _resident(proj.shape), _resident((1, d)), _resident((1, d)),
        ],
        out_specs=pl.BlockSpec((tm, d), lambda i: (i, 0)),
        compiler_params=_cparams("arbitrary"),
        name="dense_ffn",
    )(x, p, w1, w3, w2, gate, proj, g.reshape(1, d), b.reshape(1, d))


def _bias_pieces(c):
    pieces = []
    rem = c
    for _ in range(N_BIAS_PIECES - 1):
        hi = rem.astype(BF16).astype(F32)
        pieces.append(hi)
        rem = rem - hi
    pieces.append(rem)
    return pieces


def _kvq_prompt_kernel(x_ref, wk_ref, wv_ref, wq_ref, wfh_ref, wfl_ref, bf_ref,
                       k_ref, v_ref, lf_ref, qp_ref, kp_ref, vb_ref, carry_ref):
    t_idx = pl.program_id(1)
    tm = x_ref.shape[1]
    x = x_ref[0]
    xb = x.astype(BF16)
    k = _dot(xb, wk_ref[...])
    v = _dot(xb, wv_ref[...])
    q = _dot(xb, wq_ref[...]) * ATTN_SCALE
    k_ref[0] = k
    v_ref[0] = v
    vb_ref[0] = v.astype(BF16)
    logf = _log_sigmoid(_dot_x3(x, wfh_ref[...], wfl_ref[...]) + bf_ref[...])
    lf_ref[0] = logf

    @pl.when(t_idx == 0)
    def _():
        carry_ref[...] = jnp.zeros_like(carry_ref)

    row = lax.broadcasted_iota(jnp.int32, (tm, tm), 0)
    col = lax.broadcasted_iota(jnp.int32, (tm, tm), 1)
    tri = (col <= row).astype(BF16)
    c = _dot_mask(tri, logf) + carry_ref[...]
    carry_ref[...] = c[tm - 1:tm, :]
    neg = _bias_pieces(-c)

    lane = lax.broadcasted_iota(jnp.int32, (tm, LANES), 1)
    for blk in range(N_HEADS // HEADS_PER_LANE_BLOCK):
        kb = k[:, blk * LANES:(blk + 1) * LANES]
        qb = q[:, blk * LANES:(blk + 1) * LANES]
        for par in range(HEADS_PER_LANE_BLOCK):
            h = blk * HEADS_PER_LANE_BLOCK + par
            base = HEAD_DIM if par == 0 else 0
            own = (lane < HEAD_DIM) if par == 0 else (lane >= HEAD_DIM)
            extra_k = jnp.zeros((tm, LANES), F32)
            extra_q = jnp.zeros((tm, LANES), F32)
            for i, piece in enumerate(neg):
                extra_k = jnp.where(lane == base + i, piece[:, h:h + 1], extra_k)
                extra_q = jnp.where(lane == base + i, 1.0, extra_q)
            kp_ref[0, h] = jnp.where(own, kb, extra_k).astype(BF16)
            qp_ref[0, h] = jnp.where(own, qb, extra_q).astype(BF16)


def _kvq_prompt(x, wk, wv, wq, wf, bf):
    bsz, seq, d = x.shape
    tm = _pick_tile(seq, 512)
    tok = lambda bi, ti: (bi, ti, 0)
    heads = lambda bi, ti: (bi, 0, ti, 0)
    return pl.pallas_call(
        _kvq_prompt_kernel,
        out_shape=[
            jax.ShapeDtypeStruct((bsz, seq, d), F32),
            jax.ShapeDtypeStruct((bsz, seq, d), F32),
            jax.ShapeDtypeStruct((bsz, seq, N_HEADS), F32),
            jax.ShapeDtypeStruct((bsz, N_HEADS, seq, LANES), BF16),
            jax.ShapeDtypeStruct((bsz, N_HEADS, seq, LANES), BF16),
            jax.ShapeDtypeStruct((bsz, seq, d), BF16),
        ],
        grid=(bsz, seq // tm),
        in_specs=[
            pl.BlockSpec((1, tm, d), tok),
            _resident(wk.shape), _resident(wv.shape), _resident(wq.shape),
            _resident(wf[0].shape), _resident(wf[1].shape), _resident((1, N_HEADS)),
        ],
        out_specs=[
            pl.BlockSpec((1, tm, d), tok), pl.BlockSpec((1, tm, d), tok),
            pl.BlockSpec((1, tm, N_HEADS), tok),
            pl.BlockSpec((1, N_HEADS, tm, LANES), heads), pl.BlockSpec((1, N_HEADS, tm, LANES), heads),
            pl.BlockSpec((1, tm, d), tok),
        ],
        scratch_shapes=[pltpu.VMEM((1, N_HEADS), F32)],
        compiler_params=_cparams("arbitrary", "arbitrary"),
        name="kvq_prompt",
    )(x, wk, wv, wq, wf[0], wf[1], bf.reshape(1, N_HEADS))


def _kvq_sample_kernel(x_ref, wk_ref, wv_ref, wq_ref, wfh_ref, wfl_ref, bf_ref, k_ref, v_ref, lf_ref, q_ref):
    x = x_ref[...]
    xb = x.astype(BF16)
    k_ref[...] = _dot(xb, wk_ref[...])
    v_ref[...] = _dot(xb, wv_ref[...])
    q_ref[...] = _dot(xb, wq_ref[...]) * ATTN_SCALE
    lf_ref[...] = _log_sigmoid(_dot_x3(x, wfh_ref[...], wfl_ref[...]) + bf_ref[...])


def _kvq_sample(x, wk, wv, wq, wf, bf):
    n, d = x.shape
    return pl.pallas_call(
        _kvq_sample_kernel,
        out_shape=[jax.ShapeDtypeStruct((n, d), F32), jax.ShapeDtypeStruct((n, d), F32),
                   jax.ShapeDtypeStruct((n, N_HEADS), F32), jax.ShapeDtypeStruct((n, d), F32)],
        compiler_params=pltpu.CompilerParams(vmem_limit_bytes=VMEM_LIMIT),
        name="kvq_sample",
    )(x, wk, wv, wq, wf[0], wf[1], bf.reshape(1, N_HEADS))


def _flash_kernel(q_ref, k_ref, v_ref, o_ref, *, tq, tk):
    qi = pl.program_id(2)
    heads = range(HEADS_PER_LANE_BLOCK)
    qs = [q_ref[0, par] for par in heads]

    def step(off, stats, first_col):
        v = v_ref[0, pl.ds(off, tk), :]
        new = []
        for par in heads:
            m, l, acc = stats[par]
            k = k_ref[0, par, pl.ds(off, tk), :]
            s = lax.dot_general(qs[par], k, (((1,), (1,)), ((), ())), preferred_element_type=F32)
            if first_col is not None:
                rowi = lax.broadcasted_iota(jnp.int32, (tq, tk), 0)
                coli = lax.broadcasted_iota(jnp.int32, (tq, tk), 1) + first_col
                s = jnp.where(coli <= rowi, s, NEG_INF)
            m_new = jnp.maximum(m, jnp.max(s, axis=1, keepdims=True))
            alpha = jnp.exp(m - m_new)
            p = jnp.exp(s - m_new)
            l = alpha * l + jnp.sum(p, axis=1, keepdims=True)
            acc = alpha * acc + _dot(p.astype(BF16), v)
            new.append((m_new, l, acc))
        return tuple(new)

    stats = tuple((jnp.full((tq, 1), NEG_INF, F32), jnp.zeros((tq, 1), F32), jnp.zeros((tq, LANES), F32))
                  for _ in heads)
    stats = lax.fori_loop(0, qi * (tq // tk), lambda j, st: step(pl.multiple_of(j * tk, tk), st, None), stats)
    for jj in range(tq // tk):
        stats = step(pl.multiple_of(qi * tq + jj * tk, tk), stats, jj * tk)
    (_, l0, acc0), (_, l1, acc1) = stats
    lane = lax.broadcasted_iota(jnp.int32, (tq, LANES), 1)
    o_ref[0] = jnp.where(lane < HEAD_DIM, acc0 / l0, acc1 / l1).astype(o_ref.dtype)


FLASH_TQ = 512
FLASH_TK = 512


def _flash_prompt(qp, kp, vb):
    bsz, _, seq, _ = qp.shape
    d = vb.shape[2]
    tq = _pick_tile(seq, FLASH_TQ)
    tk = _pick_tile(tq, FLASH_TK)
    nblk = N_HEADS // HEADS_PER_LANE_BLOCK
    return pl.pallas_call(
        functools.partial(_flash_kernel, tq=tq, tk=tk),
        out_shape=jax.ShapeDtypeStruct((bsz, seq, d), BF16),
        grid=(bsz, nblk, seq // tq),
        in_specs=[
            pl.BlockSpec((1, HEADS_PER_LANE_BLOCK, tq, LANES), lambda bi, hb, qi: (bi, hb, qi, 0)),
            pl.BlockSpec((1, HEADS_PER_LANE_BLOCK, seq, LANES), lambda bi, hb, qi: (bi, hb, 0, 0)),
            pl.BlockSpec((1, seq, LANES), lambda bi, hb, qi: (bi, 0, hb)),
        ],
        out_specs=pl.BlockSpec((1, tq, LANES), lambda bi, hb, qi: (bi, qi, hb)),
        compiler_params=_cparams("arbitrary", "arbitrary", "arbitrary"),
        name="flash_prompt",
    )(qp, kp, vb)


def _page_group(n_pages):
    g = 8
    while n_pages % g:
        g //= 2
    return g


def _decode_bias_kernel(pt_ref, lfn_ref, *refs, n_group):
    del pt_ref
    lf_refs = refs[:n_group]
    o_ref = refs[n_group]
    carry_ref = refs[n_group + 1]
    page = lf_refs[0].shape[1]

    @pl.when(pl.program_id(1) == 0)
    def _():
        carry_ref[...] = lfn_ref[0]

    r = lax.broadcasted_iota(jnp.int32, (page, page), 0)
    c = lax.broadcasted_iota(jnp.int32, (page, page), 1)
    later = (c > r).astype(BF16)
    for g in range(n_group):
        lf = lf_refs[g][0]
        o_ref[0, n_group - 1 - g] = _dot_mask(later, lf) + carry_ref[...]
        carry_ref[...] = carry_ref[...] + jnp.sum(lf, axis=0, keepdims=True)


def _decode_bias(lf_new, cache_logf, page_table):
    bsz, n_pages = page_table.shape
    page = cache_logf.shape[1]
    n_group = _page_group(n_pages)
    steps = n_pages // n_group

    def page_map(g):
        return lambda bi, si, pt: (pt[bi, (steps - 1 - si) * n_group + (n_group - 1 - g)], 0, 0)

    return pl.pallas_call(
        functools.partial(_decode_bias_kernel, n_group=n_group),
        out_shape=jax.ShapeDtypeStruct((bsz, n_pages, page, N_HEADS), F32),
        grid_spec=pltpu.PrefetchScalarGridSpec(
            num_scalar_prefetch=1,
            grid=(bsz, steps),
            in_specs=[pl.BlockSpec((1, 1, N_HEADS), lambda bi, si, pt: (bi, 0, 0))]
            + [pl.BlockSpec((1, page, N_HEADS), page_map(g)) for g in range(n_group)],
            out_specs=pl.BlockSpec((1, n_group, page, N_HEADS), lambda bi, si, pt: (bi, steps - 1 - si, 0, 0)),
            scratch_shapes=[pltpu.VMEM((1, N_HEADS), F32)],
        ),
        compiler_params=_cparams("arbitrary", "arbitrary"),
        name="decode_bias",
    )(page_table, lf_new.reshape(bsz, 1, N_HEADS), *([cache_logf] * n_group))


def _decode_kernel(pt_ref, q_ref, kn_ref, vn_ref, bias_ref, *refs, n_group):
    del pt_ref
    k_refs = refs[:n_group]
    v_refs = refs[n_group:2 * n_group]
    o_ref = refs[2 * n_group]
    m_ref, l_ref, acc_ref = refs[2 * n_group + 1:]
    step = pl.program_id(1)
    page = k_refs[0].shape[1]
    rows = page * N_HEADS
    qb = q_ref[0].astype(BF16)

    @pl.when(step == 0)
    def _():
        m_ref[...] = jnp.full_like(m_ref, NEG_INF)
        l_ref[...] = jnp.zeros_like(l_ref)
        acc_ref[...] = jnp.zeros_like(acc_ref)

    row_head = lax.broadcasted_iota(jnp.int32, (N_HEADS, rows), 1) % N_HEADS
    own = row_head == lax.broadcasted_iota(jnp.int32, (N_HEADS, rows), 0)
    scores = []
    for g in range(n_group):
        k2 = k_refs[g][0].reshape(rows, HEAD_DIM).astype(BF16)
        s = lax.dot_general(qb, k2, (((1,), (1,)), ((), ())), preferred_element_type=F32)
        scores.append(jnp.where(own, s + bias_ref[0, g], NEG_INF))
    m = m_ref[...]
    m_new = m
    for s in scores:
        m_new = jnp.maximum(m_new, jnp.max(s, axis=1, keepdims=True))
    alpha = jnp.exp(m - m_new)
    l = alpha * l_ref[...]
    acc = alpha * acc_ref[...]
    for g in range(n_group):
        p = jnp.exp(scores[g] - m_new)
        l = l + jnp.sum(p, axis=1, keepdims=True)
        v2 = v_refs[g][0].reshape(rows, HEAD_DIM).astype(BF16)
        acc = acc + _dot(p.astype(BF16), v2)
    m_ref[...] = m_new
    l_ref[...] = l
    acc_ref[...] = acc

    @pl.when(step == pl.num_programs(1) - 1)
    def _():
        kn = kn_ref[0].astype(BF16).astype(F32)
        s_new = jnp.sum(qb.astype(F32) * kn, axis=1, keepdims=True)
        m_fin = jnp.maximum(m_new, s_new)
        a_fin = jnp.exp(m_new - m_fin)
        p_new = jnp.exp(s_new - m_fin).astype(BF16).astype(F32)
        l_fin = a_fin * l + p_new
        o_ref[0] = (a_fin * acc + p_new * vn_ref[0].astype(BF16).astype(F32)) / l_fin


def _decode_attention(q, k_new, v_new, bias, cache_k, cache_v, page_table):
    bsz = q.shape[0]
    n_pages = page_table.shape[1]
    page = cache_k.shape[1]
    n_group = _page_group(n_pages)
    steps = n_pages // n_group

    def page_map(g):
        return lambda bi, si, pt: (pt[bi, si * n_group + g], 0, 0, 0)

    row = lambda bi, si, pt: (bi, 0, 0)
    head_block = pl.BlockSpec((1, N_HEADS, HEAD_DIM), row)
    in_specs = [head_block, head_block, head_block,
                pl.BlockSpec((1, n_group, 1, page * N_HEADS), lambda bi, si, pt: (bi, si, 0, 0))]
    in_specs += [pl.BlockSpec((1, page, N_HEADS, HEAD_DIM), page_map(g)) for g in range(n_group)]
    in_specs += [pl.BlockSpec((1, page, N_HEADS, HEAD_DIM), page_map(g)) for g in range(n_group)]
    return pl.pallas_call(
        functools.partial(_decode_kernel, n_group=n_group),
        out_shape=jax.ShapeDtypeStruct((bsz, N_HEADS, HEAD_DIM), F32),
        grid_spec=pltpu.PrefetchScalarGridSpec(
            num_scalar_prefetch=1,
            grid=(bsz, steps),
            in_specs=in_specs,
            out_specs=head_block,
            scratch_shapes=[pltpu.VMEM((N_HEADS, 1), F32), pltpu.VMEM((N_HEADS, 1), F32),
                            pltpu.VMEM((N_HEADS, HEAD_DIM), F32)],
        ),
        compiler_params=_cparams("arbitrary", "arbitrary"),
        name="decode_attention",
    )(page_table, q, k_new, v_new, bias, *([cache_k] * n_group), *([cache_v] * n_group))


def _attn_out_router_kernel(x_ref, a_ref, wo_ref, g_ref, b_ref, rw_ref, rwl_ref,
                            x3_ref, idx_ref, gate_ref, rank_ref, cnt_ref, carry_ref):
    i = pl.program_id(0)
    tm = x_ref.shape[0]
    n_exp = rw_ref.shape[1]
    y = _dot(a_ref[...].astype(BF16), wo_ref[...])
    x3 = _layer_norm(DN_ALPHA * x_ref[...] + y, g_ref[...], b_ref[...])
    x3_ref[...] = x3

    logits = _dot_x3(x3, rw_ref[...], rwl_ref[...])
    eid = lax.broadcasted_iota(jnp.int32, (tm, n_exp), 1).astype(F32)
    m1 = jnp.max(logits, axis=1, keepdims=True)
    i1 = jnp.min(jnp.where(logits == m1, eid, float(n_exp)), axis=1, keepdims=True)
    rest = jnp.where(eid == i1, NEG_INF, logits)
    m2 = jnp.max(rest, axis=1, keepdims=True)
    i2 = jnp.min(jnp.where(rest == m2, eid, float(n_exp)), axis=1, keepdims=True)
    e2 = jnp.exp(m2 - m1)
    g1 = 1.0 / (1.0 + e2)
    first = lax.broadcasted_iota(jnp.int32, (tm, TOP_K), 1) == 0
    idx_ref[...] = jnp.where(first, i1, i2).astype(jnp.int32)
    gate_ref[...] = jnp.where(first, g1, e2 * g1)

    @pl.when(i == 0)
    def _():
        carry_ref[...] = jnp.zeros_like(carry_ref)

    oh1 = (eid == i1).astype(F32)
    oh2 = (eid == i2).astype(F32)
    both = oh1 + oh2
    row = lax.broadcasted_iota(jnp.int32, (tm, tm), 0)
    col = lax.broadcasted_iota(jnp.int32, (tm, tm), 1)
    before = (col < row).astype(BF16)
    excl = _dot(before, both.astype(BF16)) + carry_ref[...]
    r1 = jnp.sum(excl * oh1, axis=1, keepdims=True)
    r2 = jnp.sum(excl * oh2, axis=1, keepdims=True)
    rank_ref[...] = jnp.where(first, r1, r2).astype(jnp.int32)
    total = carry_ref[...] + jnp.sum(both, axis=0, keepdims=True)
    carry_ref[...] = total
    cnt_ref[...] = total.astype(jnp.int32)


def _attn_out_router(x, attn, wo, g, b, router_w):
    t, d = x.shape
    n_exp = router_w[0].shape[1]
    tm = _pick_tile(t, 512)
    tok = lambda i: (i, 0)
    return pl.pallas_call(
        _attn_out_router_kernel,
        out_shape=[jax.ShapeDtypeStruct((t, d), F32), jax.ShapeDtypeStruct((t, TOP_K), jnp.int32),
                   jax.ShapeDtypeStruct((t, TOP_K), F32), jax.ShapeDtypeStruct((t, TOP_K), jnp.int32),
                   jax.ShapeDtypeStruct((1, n_exp), jnp.int32)],
        grid=(t // tm,),
        in_specs=[pl.BlockSpec((tm, d), tok), pl.BlockSpec((tm, d), tok),
                  _resident(wo.shape), _resident((1, d)), _resident((1, d)),
                  _resident(router_w[0].shape), _resident(router_w[1].shape)],
        out_specs=[pl.BlockSpec((tm, d), tok), pl.BlockSpec((tm, TOP_K), tok), pl.BlockSpec((tm, TOP_K), tok),
                   pl.BlockSpec((tm, TOP_K), tok), pl.BlockSpec((1, n_exp), lambda i: (0, 0))],
        scratch_shapes=[pltpu.VMEM((1, n_exp), F32)],
        compiler_params=_cparams("arbitrary"),
        name="attn_out_router",
    )(x, attn, wo, g.reshape(1, d), b.reshape(1, d), router_w[0], router_w[1])


def _dispatch_kernel(pos_ref, x_ref, o_hbm, sem):
    tr = x_ref.shape[0]

    def issue(r, carry):
        for c in range(TOP_K):
            dst = pos_ref[0, 0, r * TOP_K + c]
            pltpu.make_async_copy(x_ref.at[pl.ds(r, 1)], o_hbm.at[pl.ds(dst, 1)], sem).start()
        return carry

    lax.fori_loop(0, tr, issue, 0, unroll=8)
    for _ in range(TOP_K):
        pltpu.make_async_copy(x_ref, o_hbm.at[pl.ds(0, tr)], sem).wait()


def _dispatch(x, pos):
    t, d = x.shape
    tr = _pick_tile(t, 256)
    pos3 = pos.reshape(t // tr, 1, tr * TOP_K)
    return pl.pallas_call(
        _dispatch_kernel,
        out_shape=jax.ShapeDtypeStruct((t * TOP_K, d), x.dtype),
        grid=(t // tr,),
        in_specs=[pl.BlockSpec((1, 1, tr * TOP_K), lambda i: (i, 0, 0), memory_space=pltpu.SMEM),
                  pl.BlockSpec((tr, d), lambda i: (i, 0))],
        out_specs=pl.BlockSpec(memory_space=pl.ANY),
        scratch_shapes=[pltpu.SemaphoreType.DMA],
        compiler_params=_cparams("arbitrary"),
        name="moe_dispatch",
    )(pos3, x)


def _expert_ffn_kernel(tile_ref, exp_ref, lo_ref, hi_ref, x_ref, w1_ref, w3_ref, w2_ref, o_ref, *, ck):
    del tile_ref, exp_ref
    vi = pl.program_id(0)
    tm = x_ref.shape[0]
    lo = lo_ref[vi]
    hi = hi_ref[vi]

    @pl.when(hi > lo)
    def _():
        y = _swiglu(x_ref[...].astype(BF16), w1_ref, w3_ref, w2_ref, ck, lead=(0,))
        row = lax.broadcasted_iota(jnp.int32, (tm, 1), 0)
        mine = (row >= lo) & (row < hi)

        @pl.when(lo == 0)
        def _():
            o_ref[...] = jnp.where(mine, y, 0.0)

        @pl.when(lo > 0)
        def _():
            o_ref[...] = jnp.where(mine, y, o_ref[...])


def _expert_ffn(xs, w1, w3, w2, visits, tm):
    vt, ve, vlo, vhi = visits
    n, d = xs.shape
    ck = _pick_chunk(w1.shape[2], 512)
    wmap = lambda i, vt, ve, vlo, vhi: (ve[i], 0, 0)
    tmap = lambda i, vt, ve, vlo, vhi: (vt[i], 0)
    return pl.pallas_call(
        functools.partial(_expert_ffn_kernel, ck=ck),
        out_shape=jax.ShapeDtypeStruct((n, d), F32),
        grid_spec=pltpu.PrefetchScalarGridSpec(
            num_scalar_prefetch=4,
            grid=(vt.shape[0],),
            in_specs=[pl.BlockSpec((tm, d), tmap),
                      pl.BlockSpec((1,) + w1.shape[1:], wmap, pipeline_mode=pl.Buffered(1)),
                      pl.BlockSpec((1,) + w3.shape[1:], wmap, pipeline_mode=pl.Buffered(1)),
                      pl.BlockSpec((1,) + w2.shape[1:], wmap, pipeline_mode=pl.Buffered(1))],
            out_specs=pl.BlockSpec((tm, d), tmap),
        ),
        compiler_params=_cparams("arbitrary"),
        name="expert_ffn",
    )(vt, ve, vlo, vhi, xs, w1, w3, w2)


def _dense_experts_kernel(x_ref, c_ref, w1_ref, w3_ref, w2_ref, o_ref, *, ck):
    e = pl.program_id(0)
    y = c_ref[0] * _swiglu(x_ref[...].astype(BF16), w1_ref, w3_ref, w2_ref, ck, lead=(0,))

    @pl.when(e == 0)
    def _():
        o_ref[...] = y

    @pl.when(e > 0)
    def _():
        o_ref[...] = o_ref[...] + y


def _dense_experts(x, combine_t, w1, w3, w2):
    n, d = x.shape
    n_exp = w1.shape[0]
    ck = _pick_chunk(w1.shape[2], 512)
    wmap = lambda e: (e, 0, 0)
    return pl.pallas_call(
        functools.partial(_dense_experts_kernel, ck=ck),
        out_shape=jax.ShapeDtypeStruct((n, d), F32),
        grid=(n_exp,),
        in_specs=[pl.BlockSpec((n, d), lambda e: (0, 0)), pl.BlockSpec((1, n, 1), wmap),
                  pl.BlockSpec((1,) + w1.shape[1:], wmap), pl.BlockSpec((1,) + w3.shape[1:], wmap),
                  pl.BlockSpec((1,) + w2.shape[1:], wmap)],
        out_specs=pl.BlockSpec((n, d), lambda e: (0, 0)),
        compiler_params=_cparams("arbitrary"),
        name="dense_experts",
    )(x, combine_t.reshape(n_exp, n, 1), w1, w3, w2)


def _combine_ple_ln_kernel(pos_ref, x_ref, gt_ref, p_ref, gate_ref, proj_ref, g_ref, b_ref, y_hbm,
                           o_ref, buf_ref, sem):
    tm = x_ref.shape[0]

    def issue(r, carry):
        for c in range(TOP_K):
            src = pos_ref[0, 0, r * TOP_K + c]
            pltpu.make_async_copy(y_hbm.at[pl.ds(src, 1)], buf_ref.at[c, pl.ds(r, 1)], sem).start()
        return carry

    lax.fori_loop(0, tm, issue, 0, unroll=8)
    for c in range(TOP_K):
        pltpu.make_async_copy(y_hbm.at[pl.ds(0, tm)], buf_ref.at[c], sem).wait()
    gt = gt_ref[...]
    f = None
    for c in range(TOP_K):
        part = gt[:, c:c + 1] * buf_ref[c]
        f = part if f is None else f + part
    o_ref[...] = _ple_ln(DN_ALPHA * x_ref[...] + f, p_ref[...], gate_ref, proj_ref, g_ref, b_ref)


def _combine_ple_ln(x, ys, pos, gates, p, gate_w, proj, g, b):
    t, d = x.shape
    tm = _pick_tile(t, 256)
    pos3 = pos.reshape(t // tm, 1, tm * TOP_K)
    tok = lambda i: (i, 0)
    return pl.pallas_call(
        _combine_ple_ln_kernel,
        out_shape=jax.ShapeDtypeStruct((t, d), F32),
        grid=(t // tm,),
        in_specs=[pl.BlockSpec((1, 1, tm * TOP_K), lambda i: (i, 0, 0), memory_space=pltpu.SMEM),
                  pl.BlockSpec((tm, d), tok), pl.BlockSpec((tm, TOP_K), tok),
                  pl.BlockSpec((tm, p.shape[1]), tok),
                  _resident(gate_w.shape), _resident(proj.shape), _resident((1, d)), _resident((1, d)),
                  pl.BlockSpec(memory_space=pl.ANY)],
        out_specs=pl.BlockSpec((tm, d), tok),
        scratch_shapes=[pltpu.VMEM((TOP_K, tm, d), F32), pltpu.SemaphoreType.DMA],
        compiler_params=_cparams("arbitrary"),
        name="moe_combine_ple_ln",
    )(pos3, x, gates, p, gate_w, proj, g.reshape(1, d), b.reshape(1, d), ys)


def _res_ple_ln_kernel(x_ref, f_ref, p_ref, gate_ref, proj_ref, g_ref, b_ref, o_ref):
    o_ref[...] = _ple_ln(DN_ALPHA * x_ref[...] + f_ref[...], p_ref[...], gate_ref, proj_ref, g_ref, b_ref)


def _res_ple_ln(x, f, p, gate_w, proj, g, b):
    n, d = x.shape
    return pl.pallas_call(
        _res_ple_ln_kernel,
        out_shape=jax.ShapeDtypeStruct((n, d), F32),
        compiler_params=pltpu.CompilerParams(vmem_limit_bytes=VMEM_LIMIT),
        name="res_ple_ln",
    )(x, f, p, gate_w, proj, g.reshape(1, d), b.reshape(1, d))


def _visit_tables(counts, n_rows, tm):
    n_exp = counts.shape[0]
    n_tiles = n_rows // tm
    n_vis = n_tiles + n_exp - 1
    end = jnp.cumsum(counts)
    start = end - counts
    first_tile = start // tm
    last_tile = jnp.where(counts > 0, (end - 1) // tm, first_tile - 1)
    n_e = last_tile - first_tile + 1
    vis_end = jnp.cumsum(n_e)
    vis_start = vis_end - n_e
    v = jnp.arange(n_vis, dtype=jnp.int32)
    e = jnp.sum((v[:, None] >= vis_end[None, :]).astype(jnp.int32), axis=1)
    valid = e < n_exp
    last_e = jnp.max(jnp.where(counts > 0, jnp.arange(n_exp, dtype=jnp.int32), 0))
    e = jnp.where(valid, e, last_e).astype(jnp.int32)
    tile = jnp.where(valid, first_tile[e] + (v - vis_start[e]), n_tiles - 1).astype(jnp.int32)
    lo = jnp.maximum(start[e], tile * tm) - tile * tm
    hi = jnp.minimum(end[e], (tile + 1) * tm) - tile * tm
    lo = jnp.where(valid, lo, 0).astype(jnp.int32)
    hi = jnp.where(valid, hi, 0).astype(jnp.int32)
    return tile, e, lo, hi


def kernel(x_prompt, x_sample, state_pool, cache_k, cache_v, cache_logf, page_table, p_prompt, p_sample,
           ln_g, ln_b, pool_w, pool_scale, w_q, w_o, w_k, w_v, w_f, b_f, ffn_w1, ffn_w3, ffn_w2,
           router_w, moe_w1, moe_w3, moe_w2, ple_proj, ple_gate):
    bp, sp, d = x_prompt.shape
    bs, ts, _ = x_sample.shape
    assert ts == 1 and d == N_HEADS * HEAD_DIM
    tp = bp * sp
    page = cache_k.shape[1]
    past_len = page_table.shape[1] * page
    bf = lambda w: w.astype(BF16)

    pw = bf(pool_w[0])
    x1p = _pool_ln_prompt(x_prompt, pw, pool_scale[0], ln_g[0, 0], ln_b[0, 0])
    xs0 = x_sample.reshape(bs, d)
    x1s = _pool_ln_sample(xs0, jnp.swapaxes(state_pool[0], 0, 1), pw, pool_scale[0], ln_g[0, 0], ln_b[0, 0],
                          past_len)
    pool_p = x_prompt[:, sp - POOL_STATE:, :][None]
    pool_s = jnp.concatenate([state_pool[0][:, 1:], x_sample], axis=1)[None]

    w1, w3, w2 = bf(ffn_w1[0]), bf(ffn_w3[0]), bf(ffn_w2[0])
    gate0, proj0 = bf(ple_gate[0]), bf(ple_proj[0])
    x2p = _dense_ffn(x1p.reshape(tp, d), p_prompt[0].reshape(tp, -1), w1, w3, w2, gate0, proj0,
                     ln_g[0, 1], ln_b[0, 1])
    x2s = _dense_ffn(x1s, p_sample[0].reshape(bs, -1), w1, w3, w2, gate0, proj0, ln_g[0, 1], ln_b[0, 1])

    wk, wv, wq, wo = bf(w_k), bf(w_v), bf(w_q[0]), bf(w_o[0])
    wf = _split_weight(w_f)
    k_p, v_p, lf_p, qp, kp, vb = _kvq_prompt(x2p.reshape(bp, sp, d), wk, wv, wq, wf, b_f)
    attn_p = _flash_prompt(qp, kp, vb).reshape(tp, d)

    k_s, v_s, lf_s, q_s = _kvq_sample(x2s, wk, wv, wq, wf, b_f)
    hd = (N_HEADS, HEAD_DIM)
    bias_s = _decode_bias(lf_s, cache_logf, page_table)
    bias_s = bias_s.reshape(bs, page_table.shape[1], 1, page * N_HEADS)
    attn_s = _decode_attention(q_s.reshape(bs, *hd), k_s.reshape(bs, *hd), v_s.reshape(bs, *hd), bias_s,
                               cache_k, cache_v, page_table).reshape(bs, d)

    gate1, proj1 = bf(ple_gate[1]), bf(ple_proj[1])
    m1, m3, m2 = bf(moe_w1[0]), bf(moe_w3[0]), bf(moe_w2[0])
    n_exp = m1.shape[0]

    rw = _split_weight(router_w[0])
    x3p, idx_p, gates_p, rank_p, counts = _attn_out_router(x2p, attn_p, wo, ln_g[1, 0], ln_b[1, 0], rw)
    counts = counts.reshape(n_exp)
    offsets = jnp.cumsum(counts) - counts
    pos = offsets[idx_p] + rank_p
    xs_sorted = _dispatch(x3p, pos)
    tm_moe = _pick_tile(tp * TOP_K, 512)
    ys_sorted = _expert_ffn(xs_sorted, m1, m3, m2, _visit_tables(counts, tp * TOP_K, tm_moe), tm_moe)
    y_p = _combine_ple_ln(x3p, ys_sorted, pos, gates_p, p_prompt[1].reshape(tp, -1), gate1, proj1,
                          ln_g[1, 1], ln_b[1, 1])

    x3s, idx_s, gates_s, _, _ = _attn_out_router(x2s, attn_s, wo, ln_g[1, 0], ln_b[1, 0], rw)
    combine_t = jnp.sum(jnp.where(jnp.arange(n_exp)[:, None, None] == idx_s[None], gates_s[None], 0.0), axis=2)
    f_s = _dense_experts(x3s, combine_t, m1, m3, m2)
    y_s = _res_ple_ln(x3s, f_s, p_sample[1].reshape(bs, -1), gate1, proj1, ln_g[1, 1], ln_b[1, 1])

    return (y_p.reshape(bp, sp, d), y_s.reshape(bs, 1, d), pool_p, pool_s,
            k_p.reshape(bp, sp, *hd), v_p.reshape(bp, sp, *hd), lf_p,
            k_s.reshape(bs, 1, *hd), v_s.reshape(bs, 1, *hd), lf_s.reshape(bs, 1, N_HEADS))
```

```python
import functools
import math

import jax
import jax.numpy as jnp
from jax import lax
from jax.experimental import pallas as pl
from jax.experimental.pallas import tpu as pltpu

F32 = jnp.float32
BF16 = jnp.bfloat16

N_HEADS = 16
HEAD_DIM = 64
HEADS_PER_LANE_BLOCK = 2
LANES = 128
POOL_WINDOWS = (2, 4, 8, 16)
POOL_STATE = max(POOL_WINDOWS) - 1
POOL_HALO = 16
DEPTH = 2
DN_ALPHA = (2.0 * DEPTH) ** 0.25
LN_EPS = 1e-5
ATTN_SCALE = 1.0 / math.sqrt(HEAD_DIM)
TOP_K = 2
N_BIAS_PIECES = 3
NEG_INF = float("-inf")
VMEM_LIMIT = 56 * 1024 * 1024


def _cparams(*sem):
    return pltpu.CompilerParams(dimension_semantics=sem, vmem_limit_bytes=VMEM_LIMIT)


def _resident(shape):
    nd = len(shape)
    return pl.BlockSpec(shape, lambda *_: (0,) * nd, pipeline_mode=pl.Buffered(1))


def _dot(a, b):
    return jnp.dot(a, b, preferred_element_type=F32)


def _split_bf16(x, n):
    pieces = []
    rem = x
    for _ in range(n):
        hi = rem.astype(BF16)
        pieces.append(hi)
        rem = rem - hi.astype(F32)
    return pieces


def _dot_x3(x, w_hi, w_lo):
    x_hi, x_lo = _split_bf16(x, 2)
    return _dot(x_hi, w_hi) + (_dot(x_hi, w_lo) + _dot(x_lo, w_hi))


def _dot_mask(mask_bf16, x):
    parts = [_dot(mask_bf16, piece) for piece in _split_bf16(x, 3)]
    return (parts[0] + parts[1]) + parts[2]


def _split_weight(w):
    hi = w.astype(BF16)
    return hi, (w - hi.astype(F32)).astype(BF16)


def _layer_norm(x, g, b):
    mu = jnp.mean(x, axis=-1, keepdims=True)
    xc = x - mu
    var = jnp.mean(xc * xc, axis=-1, keepdims=True)
    return xc * lax.rsqrt(var + LN_EPS) * g + b


def _sigmoid(x):
    return 1.0 / (1.0 + jnp.exp(-x))


def _log_sigmoid(z):
    return jnp.minimum(z, 0.0) - jnp.log1p(jnp.exp(-jnp.abs(z)))


def _pick_chunk(n, cap):
    best = LANES
    for c in range(LANES, min(n, cap) + 1, LANES):
        if n % c == 0:
            best = c
    return best


def _pick_tile(n, cap):
    if n <= cap:
        return n
    for t in range(cap, 7, -8):
        if n % t == 0:
            return t
    raise ValueError(f"no tile for {n}")


def _pool_groups(ext_ref, x, pos, w_ref, row0):
    t = x.shape[0]
    group = x.shape[1] // len(POOL_WINDOWS)
    ys = []
    for g, w in enumerate(POOL_WINDOWS):
        lo, hi = g * group, (g + 1) * group
        s = ext_ref[row0:row0 + t, lo:hi]
        for j in range(1, w):
            s = s + ext_ref[row0 - j:row0 - j + t, lo:hi]
        cnt = jnp.minimum(pos + 1, w).astype(F32)
        diff = s / cnt - x[:, lo:hi]
        ys.append(_dot(diff.astype(BF16), w_ref[g]))
    return jnp.concatenate(ys, axis=1)


def _pool_ln_prompt_kernel(x_ref, prev_ref, w_ref, sc_ref, g_ref, b_ref, o_ref, ext_ref):
    t_idx = pl.program_id(1)
    ts = x_ref.shape[1]
    x = x_ref[0]
    ext_ref[0:POOL_HALO, :] = jnp.where(t_idx > 0, prev_ref[0], 0.0)
    ext_ref[POOL_HALO:POOL_HALO + ts, :] = x
    pos = lax.broadcasted_iota(jnp.int32, (ts, 1), 0) + t_idx * ts
    y = _pool_groups(ext_ref, x, pos, w_ref, POOL_HALO) * sc_ref[...]
    o_ref[0] = _layer_norm(DN_ALPHA * x + y, g_ref[...], b_ref[...])


def _pool_ln_prompt(x, pool_w, pool_scale, g, b):
    bsz, seq, d = x.shape
    ts = _pick_tile(seq, 512)
    halo_blocks = ts // POOL_HALO
    return pl.pallas_call(
        _pool_ln_prompt_kernel,
        out_shape=jax.ShapeDtypeStruct(x.shape, F32),
        grid=(bsz, seq // ts),
        in_specs=[
            pl.BlockSpec((1, ts, d), lambda bi, ti: (bi, ti, 0)),
            pl.BlockSpec((1, POOL_HALO, d), lambda bi, ti: (bi, jnp.maximum(ti * halo_blocks - 1, 0), 0)),
            _resident(pool_w.shape), _resident((1, d)), _resident((1, d)), _resident((1, d)),
        ],
        out_specs=pl.BlockSpec((1, ts, d), lambda bi, ti: (bi, ti, 0)),
        scratch_shapes=[pltpu.VMEM((POOL_HALO + ts, d), F32)],
        compiler_params=_cparams("arbitrary", "arbitrary"),
        name="pool_ln_prompt",
    )(x, x, pool_w, pool_scale.reshape(1, d), g.reshape(1, d), b.reshape(1, d))


def _pool_ln_sample_kernel(x_ref, st_ref, w_ref, sc_ref, g_ref, b_ref, o_ref, *, pos0):
    x = x_ref[...]
    group = x.shape[1] // len(POOL_WINDOWS)
    ys = []
    for g, w in enumerate(POOL_WINDOWS):
        lo, hi = g * group, (g + 1) * group
        s = x[:, lo:hi]
        for j in range(1, w):
            s = s + st_ref[POOL_STATE - j][:, lo:hi]
        cnt = float(min(pos0 + 1, w))
        diff = s / cnt - x[:, lo:hi]
        ys.append(_dot(diff.astype(BF16), w_ref[g]))
    y = jnp.concatenate(ys, axis=1) * sc_ref[...]
    o_ref[...] = _layer_norm(DN_ALPHA * x + y, g_ref[...], b_ref[...])


def _pool_ln_sample(xs, state_t, pool_w, pool_scale, g, b, pos0):
    n, d = xs.shape
    return pl.pallas_call(
        functools.partial(_pool_ln_sample_kernel, pos0=pos0),
        out_shape=jax.ShapeDtypeStruct((n, d), F32),
        compiler_params=pltpu.CompilerParams(vmem_limit_bytes=VMEM_LIMIT),
        name="pool_ln_sample",
    )(xs, state_t, pool_w, pool_scale.reshape(1, d), g.reshape(1, d), b.reshape(1, d))


def _swiglu(xb, w1_ref, w3_ref, w2_ref, ck, lead=()):
    dff = w1_ref.shape[-1]
    acc = None
    for c in range(dff // ck):
        sl = slice(c * ck, (c + 1) * ck)
        a = _dot(xb, w1_ref[lead + (slice(None), sl)])
        g = _dot(xb, w3_ref[lead + (slice(None), sl)])
        h = (a * _sigmoid(a) * g).astype(BF16)
        part = _dot(h, w2_ref[lead + (sl, slice(None))])
        acc = part if acc is None else acc + part
    return acc


def _ple_ln(r, p, gate_ref, proj_ref, g_ref, b_ref):
    gate = _sigmoid(_dot(r.astype(BF16), gate_ref[...]))
    r = r + gate * _dot(p.astype(BF16), proj_ref[...])
    return _layer_norm(r, g_ref[...], b_ref[...])


def _dense_ffn_kernel(x_ref, p_ref, w1_ref, w3_ref, w2_ref, gate_ref, proj_ref, g_ref, b_ref, o_ref, *, ck):
    x = x_ref[...]
    f = _swiglu(x.astype(BF16), w1_ref, w3_ref, w2_ref, ck)
    o_ref[...] = _ple_ln(DN_ALPHA * x + f, p_ref[...], gate_ref, proj_ref, g_ref, b_ref)


def _dense_ffn(x, p, w1, w3, w2, gate, proj, g, b):
    t, d = x.shape
    tm = _pick_tile(t, 512)
    ck = _pick_chunk(w1.shape[1], 1408)
    return pl.pallas_call(
        functools.partial(_dense_ffn_kernel, ck=ck),
        out_shape=jax.ShapeDtypeStruct((t, d), F32),
        grid=(t // tm,),
        in_specs=[
            pl.BlockSpec((tm, d), lambda i: (i, 0)),
            pl.BlockSpec((tm, p.shape[1]), lambda i: (i, 0)),
            _resident(w1.shape), _resident(w3.shape), _resident(w2.shape),
            _resident(gate.shape), _resident(proj.shape), _resident((1, d)), _resident((1, d)),
        ],
        out_specs=pl.BlockSpec((tm, d), lambda i: (i, 0)),
        compiler_params=_cparams("arbitrary"),
        name="dense_ffn",
    )(x, p, w1, w3, w2, gate, proj, g.reshape(1, d), b.reshape(1, d))


def _bias_pieces(c):
    pieces = []
    rem = c
    for _ in range(N_BIAS_PIECES - 1):
        hi = rem.astype(BF16).astype(F32)
        pieces.append(hi)
        rem = rem - hi
    pieces.append(rem)
    return pieces


def _kvq_prompt_kernel(x_ref, wk_ref, wv_ref, wq_ref, wfh_ref, wfl_ref, bf_ref,
                       k_ref, v_ref, lf_ref, qp_ref, kp_ref, vb_ref, carry_ref):
    t_idx = pl.program_id(1)
    tm = x_ref.shape[1]
    x = x_ref[0]
    xb = x.astype(BF16)
    k = _dot(xb, wk_ref[...])
    v = _dot(xb, wv_ref[...])
    q = _dot(xb, wq_ref[...]) * ATTN_SCALE
    k_ref[0] = k
    v_ref[0] = v
    vb_ref[0] = v.astype(BF16)
    logf = _log_sigmoid(_dot_x3(x, wfh_ref[...], wfl_ref[...]) + bf_ref[...])
    lf_ref[0] = logf

    @pl.when(t_idx == 0)
    def _():
        carry_ref[...] = jnp.zeros_like(carry_ref)

    row = lax.broadcasted_iota(jnp.int32, (tm, tm), 0)
    col = lax.broadcasted_iota(jnp.int32, (tm, tm), 1)
    tri = (col <= row).astype(BF16)
    c = _dot_mask(tri, logf) + carry_ref[...]
    carry_ref[...] = c[tm - 1:tm, :]
    neg = _bias_pieces(-c)

    lane = lax.broadcasted_iota(jnp.int32, (tm, LANES), 1)
    for blk in range(N_HEADS // HEADS_PER_LANE_BLOCK):
        kb = k[:, blk * LANES:(blk + 1) * LANES]
        qb = q[:, blk * LANES:(blk + 1) * LANES]
        for par in range(HEADS_PER_LANE_BLOCK):
            h = blk * HEADS_PER_LANE_BLOCK + par
            base = HEAD_DIM if par == 0 else 0
            own = (lane < HEAD_DIM) if par == 0 else (lane >= HEAD_DIM)
            extra_k = jnp.zeros((tm, LANES), F32)
            extra_q = jnp.zeros((tm, LANES), F32)
            for i, piece in enumerate(neg):
                extra_k = jnp.where(lane == base + i, piece[:, h:h + 1], extra_k)
                extra_q = jnp.where(lane == base + i, 1.0, extra_q)
            kp_ref[0, h] = jnp.where(own, kb, extra_k).astype(BF16)
            qp_ref[0, h] = jnp.where(own, qb, extra_q).astype(BF16)


def _kvq_prompt(x, wk, wv, wq, wf, bf):
    bsz, seq, d = x.shape
    tm = _pick_tile(seq, 512)
    tok = lambda bi, ti: (bi, ti, 0)
    heads = lambda bi, ti: (bi, 0, ti, 0)
    return pl.pallas_call(
        _kvq_prompt_kernel,
        out_shape=[
            jax.ShapeDtypeStruct((bsz, seq, d), F32),
            jax.ShapeDtypeStruct((bsz, seq, d), F32),
            jax.ShapeDtypeStruct((bsz, seq, N_HEADS), F32),
            jax.ShapeDtypeStruct((bsz, N_HEADS, seq, LANES), BF16),
            jax.ShapeDtypeStruct((bsz, N_HEADS, seq, LANES), BF16),
            jax.ShapeDtypeStruct((bsz, seq, d), BF16),
        ],
        grid=(bsz, seq // tm),
        in_specs=[
            pl.BlockSpec((1, tm, d), tok),
            _resident(wk.shape), _resident(wv.shape), _resident(wq.shape),
            _resident(wf[0].shape), _resident(wf[1].shape), _resident((1, N_HEADS)),
        ],
        out_specs=[
            pl.BlockSpec((1, tm, d), tok), pl.BlockSpec((1, tm, d), tok),
            pl.BlockSpec((1, tm, N_HEADS), tok),
            pl.BlockSpec((1, N_HEADS, tm, LANES), heads), pl.BlockSpec((1, N_HEADS, tm, LANES), heads),
            pl.BlockSpec((1, tm, d), tok),
        ],
        scratch_shapes=[pltpu.VMEM((1, N_HEADS), F32)],
        compiler_params=_cparams("arbitrary", "arbitrary"),
        name="kvq_prompt",
    )(x, wk, wv, wq, wf[0], wf[1], bf.reshape(1, N_HEADS))


def _kvq_sample_kernel(x_ref, wk_ref, wv_ref, wq_ref, wfh_ref, wfl_ref, bf_ref, k_ref, v_ref, lf_ref, q_ref):
    x = x_ref[...]
    xb = x.astype(BF16)
    k_ref[...] = _dot(xb, wk_ref[...])
    v_ref[...] = _dot(xb, wv_ref[...])
    q_ref[...] = _dot(xb, wq_ref[...]) * ATTN_SCALE
    lf_ref[...] = _log_sigmoid(_dot_x3(x, wfh_ref[...], wfl_ref[...]) + bf_ref[...])


def _kvq_sample(x, wk, wv, wq, wf, bf):
    n, d = x.shape
    return pl.pallas_call(
        _kvq_sample_kernel,
        out_shape=[jax.ShapeDtypeStruct((n, d), F32), jax.ShapeDtypeStruct((n, d), F32),
                   jax.ShapeDtypeStruct((n, N_HEADS), F32), jax.ShapeDtypeStruct((n, d), F32)],
        compiler_params=pltpu.CompilerParams(vmem_limit_bytes=VMEM_LIMIT),
        name="kvq_sample",
    )(x, wk, wv, wq, wf[0], wf[1], bf.reshape(1, N_HEADS))


def _flash_kernel(q_ref, k_ref, v_ref, o_ref, *, tq, tk):
    qi = pl.program_id(2)
    heads = range(HEADS_PER_LANE_BLOCK)
    qs = [q_ref[0, par] for par in heads]

    def step(off, stats, first_col):
        v = v_ref[0, pl.ds(off, tk), :]
        new = []
        for par in heads:
            m, l, acc = stats[par]
            k = k_ref[0, par, pl.ds(off, tk), :]
            s = lax.dot_general(qs[par], k, (((1,), (1,)), ((), ())), preferred_element_type=F32)
            if first_col is not None:
                rowi = lax.broadcasted_iota(jnp.int32, (tq, tk), 0)
                coli = lax.broadcasted_iota(jnp.int32, (tq, tk), 1) + first_col
                s = jnp.where(coli <= rowi, s, NEG_INF)
            m_new = jnp.maximum(m, jnp.max(s, axis=1, keepdims=True))
            alpha = jnp.exp(m - m_new)
            p = jnp.exp(s - m_new)
            l = alpha * l + jnp.sum(p, axis=1, keepdims=True)
            acc = alpha * acc + _dot(p.astype(BF16), v)
            new.append((m_new, l, acc))
        return tuple(new)

    stats = tuple((jnp.full((tq, 1), NEG_INF, F32), jnp.zeros((tq, 1), F32), jnp.zeros((tq, LANES), F32))
                  for _ in heads)
    stats = lax.fori_loop(0, qi * (tq // tk), lambda j, st: step(pl.multiple_of(j * tk, tk), st, None), stats)
    for jj in range(tq // tk):
        stats = step(pl.multiple_of(qi * tq + jj * tk, tk), stats, jj * tk)
    (_, l0, acc0), (_, l1, acc1) = stats
    lane = lax.broadcasted_iota(jnp.int32, (tq, LANES), 1)
    o_ref[0] = jnp.where(lane < HEAD_DIM, acc0 / l0, acc1 / l1).astype(o_ref.dtype)


FLASH_TQ = 512
FLASH_TK = 512


def _flash_prompt(qp, kp, vb):
    bsz, _, seq, _ = qp.shape
    d = vb.shape[2]
    tq = _pick_tile(seq, FLASH_TQ)
    tk = _pick_tile(tq, FLASH_TK)
    nblk = N_HEADS // HEADS_PER_LANE_BLOCK
    return pl.pallas_call(
        functools.partial(_flash_kernel, tq=tq, tk=tk),
        out_shape=jax.ShapeDtypeStruct((bsz, seq, d), BF16),
        grid=(bsz, nblk, seq // tq),
        in_specs=[
            pl.BlockSpec((1, HEADS_PER_LANE_BLOCK, tq, LANES), lambda bi, hb, qi: (bi, hb, qi, 0)),
            pl.BlockSpec((1, HEADS_PER_LANE_BLOCK, seq, LANES), lambda bi, hb, qi: (bi, hb, 0, 0)),
            pl.BlockSpec((1, seq, LANES), lambda bi, hb, qi: (bi, 0, hb)),
        ],
        out_specs=pl.BlockSpec((1, tq, LANES), lambda bi, hb, qi: (bi, qi, hb)),
        compiler_params=_cparams("arbitrary", "arbitrary", "arbitrary"),
        name="flash_prompt",
    )(qp, kp, vb)


def _page_group(n_pages):
    g = 8
    while n_pages % g:
        g //= 2
    return g


def _decode_kernel(pt_ref, q_ref, kn_ref, vn_ref, lfn_ref, *refs, n_group):
    del pt_ref
    k_refs = refs[:n_group]
    v_refs = refs[n_group:2 * n_group]
    lf_refs = refs[2 * n_group:3 * n_group]
    o_ref = refs[3 * n_group]
    m_ref, l_ref, acc_ref, carry_ref = refs[3 * n_group + 1:]
    step = pl.program_id(1)
    page = k_refs[0].shape[3]
    q = q_ref[0]

    @pl.when(step == 0)
    def _():
        m_ref[...] = jnp.full_like(m_ref, NEG_INF)
        l_ref[...] = jnp.zeros_like(l_ref)
        acc_ref[...] = jnp.zeros_like(acc_ref)
        carry_ref[...] = lfn_ref[0]

    jr = lax.broadcasted_iota(jnp.int32, (page, page), 0)
    kc = lax.broadcasted_iota(jnp.int32, (page, page), 1)
    later = (jr > kc).astype(BF16)
    scores = []
    for g in range(n_group):
        s = jnp.sum(k_refs[g][0] * q, axis=1)
        lf = lf_refs[g][0]
        parts = [_dot(piece, later) for piece in _split_bf16(lf, 3)]
        bias = (parts[0] + parts[1]) + parts[2] + carry_ref[...]
        carry_ref[...] = carry_ref[...] + jnp.sum(lf, axis=1, keepdims=True)
        scores.append(s + bias)
    m = m_ref[...]
    m_new = m
    for s in scores:
        m_new = jnp.maximum(m_new, jnp.max(s, axis=1, keepdims=True))
    alpha = jnp.exp(m - m_new)
    l = alpha * l_ref[...]
    acc = alpha[:, :, None] * acc_ref[...]
    for g in range(n_group):
        p = jnp.exp(scores[g] - m_new)
        l = l + jnp.sum(p, axis=1, keepdims=True)
        acc = acc + p[:, None, :] * v_refs[g][0]
    m_ref[...] = m_new
    l_ref[...] = l
    acc_ref[...] = acc

    @pl.when(step == pl.num_programs(1) - 1)
    def _():
        qb = q.astype(BF16).astype(F32)
        kn = kn_ref[0].astype(BF16).astype(F32)
        s_new = jnp.sum(qb * kn, axis=1)[:, 0:1]
        m_fin = jnp.maximum(m_new, s_new)
        a_fin = jnp.exp(m_new - m_fin)
        p_new = jnp.exp(s_new - m_fin).astype(BF16).astype(F32)
        l_fin = a_fin * l + p_new
        total = jnp.sum(acc, axis=2, keepdims=True)
        vn = vn_ref[0].astype(BF16).astype(F32)
        o_ref[0] = (a_fin[:, :, None] * total + p_new[:, :, None] * vn) / l_fin[:, :, None]


def _decode_attention(q, k_new, v_new, lf_new, cache_kt, cache_vt, cache_lft, page_table):
    bsz = q.shape[0]
    n_pages = page_table.shape[1]
    page = cache_kt.shape[3]
    n_group = _page_group(n_pages)
    steps = n_pages // n_group

    def page_map(g, nd):
        return lambda bi, si, pt: (pt[bi, n_pages - 1 - (si * n_group + g)],) + (0,) * (nd - 1)

    row = lambda bi, si, pt: (bi, 0, 0, 0)
    head_block = pl.BlockSpec((1, N_HEADS, HEAD_DIM, page), row)
    in_specs = [head_block, head_block, head_block,
                pl.BlockSpec((1, N_HEADS, 1), lambda bi, si, pt: (bi, 0, 0))]
    in_specs += [pl.BlockSpec((1, N_HEADS, HEAD_DIM, page), page_map(g, 4)) for g in range(n_group)]
    in_specs += [pl.BlockSpec((1, N_HEADS, HEAD_DIM, page), page_map(g, 4)) for g in range(n_group)]
    in_specs += [pl.BlockSpec((1, N_HEADS, page), page_map(g, 3)) for g in range(n_group)]
    return pl.pallas_call(
        functools.partial(_decode_kernel, n_group=n_group),
        out_shape=jax.ShapeDtypeStruct((bsz, N_HEADS, HEAD_DIM, page), F32),
        grid_spec=pltpu.PrefetchScalarGridSpec(
            num_scalar_prefetch=1,
            grid=(bsz, steps),
            in_specs=in_specs,
            out_specs=head_block,
            scratch_shapes=[pltpu.VMEM((N_HEADS, 1), F32), pltpu.VMEM((N_HEADS, 1), F32),
                            pltpu.VMEM((N_HEADS, HEAD_DIM, page), F32), pltpu.VMEM((N_HEADS, 1), F32)],
        ),
        compiler_params=_cparams("arbitrary", "arbitrary"),
        name="decode_attention",
    )(page_table, q, k_new, v_new, lf_new.reshape(bsz, N_HEADS, 1), *([cache_kt] * n_group),
      *([cache_vt] * n_group), *([cache_lft] * n_group))


def _attn_out_router_kernel(x_ref, a_ref, wo_ref, g_ref, b_ref, rw_ref, rwl_ref,
                            x3_ref, idx_ref, gate_ref, rank_ref, cnt_ref, carry_ref):
    i = pl.program_id(0)
    tm = x_ref.shape[0]
    n_exp = rw_ref.shape[1]
    y = _dot(a_ref[...].astype(BF16), wo_ref[...])
    x3 = _layer_norm(DN_ALPHA * x_ref[...] + y, g_ref[...], b_ref[...])
    x3_ref[...] = x3

    logits = _dot_x3(x3, rw_ref[...], rwl_ref[...])
    eid = lax.broadcasted_iota(jnp.int32, (tm, n_exp), 1).astype(F32)
    m1 = jnp.max(logits, axis=1, keepdims=True)
    i1 = jnp.min(jnp.where(logits == m1, eid, float(n_exp)), axis=1, keepdims=True)
    rest = jnp.where(eid == i1, NEG_INF, logits)
    m2 = jnp.max(rest, axis=1, keepdims=True)
    i2 = jnp.min(jnp.where(rest == m2, eid, float(n_exp)), axis=1, keepdims=True)
    e2 = jnp.exp(m2 - m1)
    g1 = 1.0 / (1.0 + e2)
    first = lax.broadcasted_iota(jnp.int32, (tm, TOP_K), 1) == 0
    idx_ref[...] = jnp.where(first, i1, i2).astype(jnp.int32)
    gate_ref[...] = jnp.where(first, g1, e2 * g1)

    @pl.when(i == 0)
    def _():
        carry_ref[...] = jnp.zeros_like(carry_ref)

    oh1 = (eid == i1).astype(F32)
    oh2 = (eid == i2).astype(F32)
    both = oh1 + oh2
    row = lax.broadcasted_iota(jnp.int32, (tm, tm), 0)
    col = lax.broadcasted_iota(jnp.int32, (tm, tm), 1)
    before = (col < row).astype(BF16)
    excl = _dot(before, both.astype(BF16)) + carry_ref[...]
    r1 = jnp.sum(excl * oh1, axis=1, keepdims=True)
    r2 = jnp.sum(excl * oh2, axis=1, keepdims=True)
    rank_ref[...] = jnp.where(first, r1, r2).astype(jnp.int32)
    total = carry_ref[...] + jnp.sum(both, axis=0, keepdims=True)
    carry_ref[...] = total
    cnt_ref[...] = total.astype(jnp.int32)


def _attn_out_router(x, attn, wo, g, b, router_w):
    t, d = x.shape
    n_exp = router_w[0].shape[1]
    tm = _pick_tile(t, 512)
    tok = lambda i: (i, 0)
    return pl.pallas_call(
        _attn_out_router_kernel,
        out_shape=[jax.ShapeDtypeStruct((t, d), F32), jax.ShapeDtypeStruct((t, TOP_K), jnp.int32),
                   jax.ShapeDtypeStruct((t, TOP_K), F32), jax.ShapeDtypeStruct((t, TOP_K), jnp.int32),
                   jax.ShapeDtypeStruct((1, n_exp), jnp.int32)],
        grid=(t // tm,),
        in_specs=[pl.BlockSpec((tm, d), tok), pl.BlockSpec((tm, d), tok),
                  _resident(wo.shape), _resident((1, d)), _resident((1, d)),
                  _resident(router_w[0].shape), _resident(router_w[1].shape)],
        out_specs=[pl.BlockSpec((tm, d), tok), pl.BlockSpec((tm, TOP_K), tok), pl.BlockSpec((tm, TOP_K), tok),
                   pl.BlockSpec((tm, TOP_K), tok), pl.BlockSpec((1, n_exp), lambda i: (0, 0))],
        scratch_shapes=[pltpu.VMEM((1, n_exp), F32)],
        compiler_params=_cparams("arbitrary"),
        name="attn_out_router",
    )(x, attn, wo, g.reshape(1, d), b.reshape(1, d), router_w[0], router_w[1])


def _dispatch_kernel(pos_ref, x_ref, o_hbm, sem):
    tr = x_ref.shape[0]

    def issue(r, carry):
        for c in range(TOP_K):
            dst = pos_ref[0, 0, r * TOP_K + c]
            pltpu.make_async_copy(x_ref.at[pl.ds(r, 1)], o_hbm.at[pl.ds(dst, 1)], sem).start()
        return carry

    lax.fori_loop(0, tr, issue, 0, unroll=8)
    for _ in range(TOP_K):
        pltpu.make_async_copy(x_ref, o_hbm.at[pl.ds(0, tr)], sem).wait()


def _dispatch(x, pos):
    t, d = x.shape
    tr = _pick_tile(t, 256)
    pos3 = pos.reshape(t // tr, 1, tr * TOP_K)
    return pl.pallas_call(
        _dispatch_kernel,
        out_shape=jax.ShapeDtypeStruct((t * TOP_K, d), x.dtype),
        grid=(t // tr,),
        in_specs=[pl.BlockSpec((1, 1, tr * TOP_K), lambda i: (i, 0, 0), memory_space=pltpu.SMEM),
                  pl.BlockSpec((tr, d), lambda i: (i, 0))],
        out_specs=pl.BlockSpec(memory_space=pl.ANY),
        scratch_shapes=[pltpu.SemaphoreType.DMA],
        compiler_params=_cparams("arbitrary"),
        name="moe_dispatch",
    )(pos3, x)


def _expert_ffn_kernel(tile_ref, exp_ref, lo_ref, hi_ref, x_ref, w1_ref, w3_ref, w2_ref, o_ref, *, ck):
    del tile_ref, exp_ref
    vi = pl.program_id(0)
    tm = x_ref.shape[0]
    lo = lo_ref[vi]
    hi = hi_ref[vi]

    @pl.when(hi > lo)
    def _():
        y = _swiglu(x_ref[...].astype(BF16), w1_ref, w3_ref, w2_ref, ck, lead=(0,))
        row = lax.broadcasted_iota(jnp.int32, (tm, 1), 0)
        mine = (row >= lo) & (row < hi)

        @pl.when(lo == 0)
        def _():
            o_ref[...] = jnp.where(mine, y, 0.0)

        @pl.when(lo > 0)
        def _():
            o_ref[...] = jnp.where(mine, y, o_ref[...])


def _expert_ffn(xs, w1, w3, w2, visits, tm):
    vt, ve, vlo, vhi = visits
    n, d = xs.shape
    ck = _pick_chunk(w1.shape[2], 512)
    wmap = lambda i, vt, ve, vlo, vhi: (ve[i], 0, 0)
    tmap = lambda i, vt, ve, vlo, vhi: (vt[i], 0)
    return pl.pallas_call(
        functools.partial(_expert_ffn_kernel, ck=ck),
        out_shape=jax.ShapeDtypeStruct((n, d), F32),
        grid_spec=pltpu.PrefetchScalarGridSpec(
            num_scalar_prefetch=4,
            grid=(vt.shape[0],),
            in_specs=[pl.BlockSpec((tm, d), tmap),
                      pl.BlockSpec((1,) + w1.shape[1:], wmap, pipeline_mode=pl.Buffered(1)),
                      pl.BlockSpec((1,) + w3.shape[1:], wmap, pipeline_mode=pl.Buffered(1)),
                      pl.BlockSpec((1,) + w2.shape[1:], wmap, pipeline_mode=pl.Buffered(1))],
            out_specs=pl.BlockSpec((tm, d), tmap),
        ),
        compiler_params=_cparams("arbitrary"),
        name="expert_ffn",
    )(vt, ve, vlo, vhi, xs, w1, w3, w2)


def _dense_experts_kernel(x_ref, c_ref, w1_ref, w3_ref, w2_ref, o_ref, *, ck):
    e = pl.program_id(0)
    y = c_ref[0] * _swiglu(x_ref[...].astype(BF16), w1_ref, w3_ref, w2_ref, ck, lead=(0,))

    @pl.when(e == 0)
    def _():
        o_ref[...] = y

    @pl.when(e > 0)
    def _():
        o_ref[...] = o_ref[...] + y


def _dense_experts(x, combine_t, w1, w3, w2):
    n, d = x.shape
    n_exp = w1.shape[0]
    ck = _pick_chunk(w1.shape[2], 512)
    wmap = lambda e: (e, 0, 0)
    return pl.pallas_call(
        functools.partial(_dense_experts_kernel, ck=ck),
        out_shape=jax.ShapeDtypeStruct((n, d), F32),
        grid=(n_exp,),
        in_specs=[pl.BlockSpec((n, d), lambda e: (0, 0)), pl.BlockSpec((1, n, 1), wmap),
                  pl.BlockSpec((1,) + w1.shape[1:], wmap), pl.BlockSpec((1,) + w3.shape[1:], wmap),
                  pl.BlockSpec((1,) + w2.shape[1:], wmap)],
        out_specs=pl.BlockSpec((n, d), lambda e: (0, 0)),
        compiler_params=_cparams("arbitrary"),
        name="dense_experts",
    )(x, combine_t.reshape(n_exp, n, 1), w1, w3, w2)


def _combine_ple_ln_kernel(pos_ref, x_ref, gt_ref, p_ref, gate_ref, proj_ref, g_ref, b_ref, y_hbm,
                           o_ref, buf_ref, sem):
    tm = x_ref.shape[0]

    def issue(r, carry):
        for c in range(TOP_K):
            src = pos_ref[0, 0, r * TOP_K + c]
            pltpu.make_async_copy(y_hbm.at[pl.ds(src, 1)], buf_ref.at[c, pl.ds(r, 1)], sem).start()
        return carry

    lax.fori_loop(0, tm, issue, 0, unroll=8)
    for c in range(TOP_K):
        pltpu.make_async_copy(y_hbm.at[pl.ds(0, tm)], buf_ref.at[c], sem).wait()
    gt = gt_ref[...]
    f = None
    for c in range(TOP_K):
        part = gt[:, c:c + 1] * buf_ref[c]
        f = part if f is None else f + part
    o_ref[...] = _ple_ln(DN_ALPHA * x_ref[...] + f, p_ref[...], gate_ref, proj_ref, g_ref, b_ref)


def _combine_ple_ln(x, ys, pos, gates, p, gate_w, proj, g, b):
    t, d = x.shape
    tm = _pick_tile(t, 256)
    pos3 = pos.reshape(t // tm, 1, tm * TOP_K)
    tok = lambda i: (i, 0)
    return pl.pallas_call(
        _combine_ple_ln_kernel,
        out_shape=jax.ShapeDtypeStruct((t, d), F32),
        grid=(t // tm,),
        in_specs=[pl.BlockSpec((1, 1, tm * TOP_K), lambda i: (i, 0, 0), memory_space=pltpu.SMEM),
                  pl.BlockSpec((tm, d), tok), pl.BlockSpec((tm, TOP_K), tok),
                  pl.BlockSpec((tm, p.shape[1]), tok),
                  _resident(gate_w.shape), _resident(proj.shape), _resident((1, d)), _resident((1, d)),
                  pl.BlockSpec(memory_space=pl.ANY)],
        out_specs=pl.BlockSpec((tm, d), tok),
        scratch_shapes=[pltpu.VMEM((TOP_K, tm, d), F32), pltpu.SemaphoreType.DMA],
        compiler_params=_cparams("arbitrary"),
        name="moe_combine_ple_ln",
    )(pos3, x, gates, p, gate_w, proj, g.reshape(1, d), b.reshape(1, d), ys)


def _res_ple_ln_kernel(x_ref, f_ref, p_ref, gate_ref, proj_ref, g_ref, b_ref, o_ref):
    o_ref[...] = _ple_ln(DN_ALPHA * x_ref[...] + f_ref[...], p_ref[...], gate_ref, proj_ref, g_ref, b_ref)


def _res_ple_ln(x, f, p, gate_w, proj, g, b):
    n, d = x.shape
    return pl.pallas_call(
        _res_ple_ln_kernel,
        out_shape=jax.ShapeDtypeStruct((n, d), F32),
        compiler_params=pltpu.CompilerParams(vmem_limit_bytes=VMEM_LIMIT),
        name="res_ple_ln",
    )(x, f, p, gate_w, proj, g.reshape(1, d), b.reshape(1, d))


def _visit_tables(counts, n_rows, tm):
    n_exp = counts.shape[0]
    n_tiles = n_rows // tm
    n_vis = n_tiles + n_exp - 1
    end = jnp.cumsum(counts)
    start = end - counts
    first_tile = start // tm
    last_tile = jnp.where(counts > 0, (end - 1) // tm, first_tile - 1)
    n_e = last_tile - first_tile + 1
    vis_end = jnp.cumsum(n_e)
    vis_start = vis_end - n_e
    v = jnp.arange(n_vis, dtype=jnp.int32)
    e = jnp.sum((v[:, None] >= vis_end[None, :]).astype(jnp.int32), axis=1)
    valid = e < n_exp
    last_e = jnp.max(jnp.where(counts > 0, jnp.arange(n_exp, dtype=jnp.int32), 0))
    e = jnp.where(valid, e, last_e).astype(jnp.int32)
    tile = jnp.where(valid, first_tile[e] + (v - vis_start[e]), n_tiles - 1).astype(jnp.int32)
    lo = jnp.maximum(start[e], tile * tm) - tile * tm
    hi = jnp.minimum(end[e], (tile + 1) * tm) - tile * tm
    lo = jnp.where(valid, lo, 0).astype(jnp.int32)
    hi = jnp.where(valid, hi, 0).astype(jnp.int32)
    return tile, e, lo, hi


def kernel(x_prompt, x_sample, state_pool, cache_k, cache_v, cache_logf, page_table, p_prompt, p_sample,
           ln_g, ln_b, pool_w, pool_scale, w_q, w_o, w_k, w_v, w_f, b_f, ffn_w1, ffn_w3, ffn_w2,
           router_w, moe_w1, moe_w3, moe_w2, ple_proj, ple_gate):
    bp, sp, d = x_prompt.shape
    bs, ts, _ = x_sample.shape
    assert ts == 1 and d == N_HEADS * HEAD_DIM
    tp = bp * sp
    page = cache_k.shape[1]
    past_len = page_table.shape[1] * page
    bf = lambda w: w.astype(BF16)

    pw = bf(pool_w[0])
    x1p = _pool_ln_prompt(x_prompt, pw, pool_scale[0], ln_g[0, 0], ln_b[0, 0])
    xs0 = x_sample.reshape(bs, d)
    x1s = _pool_ln_sample(xs0, jnp.swapaxes(state_pool[0], 0, 1), pw, pool_scale[0], ln_g[0, 0], ln_b[0, 0],
                          past_len)
    pool_p = x_prompt[:, sp - POOL_STATE:, :][None]
    pool_s = jnp.concatenate([state_pool[0][:, 1:], x_sample], axis=1)[None]

    w1, w3, w2 = bf(ffn_w1[0]), bf(ffn_w3[0]), bf(ffn_w2[0])
    gate0, proj0 = bf(ple_gate[0]), bf(ple_proj[0])
    x2p = _dense_ffn(x1p.reshape(tp, d), p_prompt[0].reshape(tp, -1), w1, w3, w2, gate0, proj0,
                     ln_g[0, 1], ln_b[0, 1])
    x2s = _dense_ffn(x1s, p_sample[0].reshape(bs, -1), w1, w3, w2, gate0, proj0, ln_g[0, 1], ln_b[0, 1])

    wk, wv, wq, wo = bf(w_k), bf(w_v), bf(w_q[0]), bf(w_o[0])
    wf = _split_weight(w_f)
    k_p, v_p, lf_p, qp, kp, vb = _kvq_prompt(x2p.reshape(bp, sp, d), wk, wv, wq, wf, b_f)
    attn_p = _flash_prompt(qp, kp, vb).reshape(tp, d)

    k_s, v_s, lf_s, q_s = _kvq_sample(x2s, wk, wv, wq, wf, b_f)
    hd = (N_HEADS, HEAD_DIM)
    lanes = lambda a: jnp.broadcast_to(a.reshape(bs, *hd, 1), (bs, *hd, page))
    attn_s = _decode_attention(lanes(q_s), lanes(k_s), lanes(v_s), lf_s,
                               jnp.transpose(cache_k, (0, 2, 3, 1)), jnp.transpose(cache_v, (0, 2, 3, 1)),
                               jnp.transpose(cache_logf, (0, 2, 1)), page_table)[..., 0].reshape(bs, d)

    gate1, proj1 = bf(ple_gate[1]), bf(ple_proj[1])
    m1, m3, m2 = bf(moe_w1[0]), bf(moe_w3[0]), bf(moe_w2[0])
    n_exp = m1.shape[0]

    rw = _split_weight(router_w[0])
    x3p, idx_p, gates_p, rank_p, counts = _attn_out_router(x2p, attn_p, wo, ln_g[1, 0], ln_b[1, 0], rw)
    counts = counts.reshape(n_exp)
    offsets = jnp.cumsum(counts) - counts
    pos = offsets[idx_p] + rank_p
    xs_sorted = _dispatch(x3p, pos)
    tm_moe = _pick_tile(tp * TOP_K, 512)
    ys_sorted = _expert_ffn(xs_sorted, m1, m3, m2, _visit_tables(counts, tp * TOP_K, tm_moe), tm_moe)
    y_p = _combine_ple_ln(x3p, ys_sorted, pos, gates_p, p_prompt[1].reshape(tp, -1), gate1, proj1,
                          ln_g[1, 1], ln_b[1, 1])

    x3s, idx_s, gates_s, _, _ = _attn_out_router(x2s, attn_s, wo, ln_g[1, 0], ln_b[1, 0], rw)
    combine_t = jnp.sum(jnp.where(jnp.arange(n_exp)[:, None, None] == idx_s[None], gates_s[None], 0.0), axis=2)
    f_s = _dense_experts(x3s, combine_t, m1, m3, m2)
    y_s = _res_ple_ln(x3s, f_s, p_sample[1].reshape(bs, -1), gate1, proj1, ln_g[1, 1], ln_b[1, 1])

    return (y_p.reshape(bp, sp, d), y_s.reshape(bs, 1, d), pool_p, pool_s,
            k_p.reshape(bp, sp, *hd), v_p.reshape(bp, sp, *hd), lf_p,
            k_s.reshape(bs, 1, *hd), v_s.reshape(bs, 1, *hd), lf_s.reshape(bs, 1, N_HEADS))
```
